```python
import math
import jax, jax.numpy as jnp
from jax import lax
import numpy as np

D_MODEL = 1024
BATCH = 2
SEQ = 8192
DEPTH = 2
DEC_BATCH = 32
DEC_SEQ = 1
PAST_LEN = 16384
PAGE_SIZE = 128

D_SSD = D_MODEL
SSD_HEAD_DIM = 64
N_SSD_HEADS = D_SSD // SSD_HEAD_DIM
SSD_GROUPS = 2
SSD_HEADS_PER_GROUP = N_SSD_HEADS // SSD_GROUPS
SSD_STATE = 128
SSD_CONV = 4
SSD_CHUNK = 128
SSD_CONV_DIM = D_SSD + 2 * SSD_GROUPS * SSD_STATE
DT_MIN = 0.001
DT_MAX = 0.1
D_CONF = D_MODEL // 2
CONF_WIDTH = 31
D_ATT = D_MODEL // 2
N_ATT_HEADS = 8
ATT_HEAD_DIM = D_ATT // N_ATT_HEADS
Q_BLOCK = 128
ATT_BIAS_INIT = -8.0
D_MIX = D_SSD + D_CONF + D_ATT
IN_WIDTHS = (D_SSD, SSD_CONV_DIM, N_SSD_HEADS, D_CONF, D_CONF, D_CONF, D_ATT, D_ATT, D_ATT, D_ATT)
D_IN_PROJ = sum(IN_WIDTHS)
EPS = 1e-6

kernel_name = 'hybrid_ssd_conformer_stickbreak_step'


def _rms(xf):
    return xf * lax.rsqrt(jnp.mean(xf * xf, axis=-1, keepdims=True) + EPS)


def rmsnorm(x, g):
    return (_rms(x.astype(jnp.float32)) * g.astype(jnp.float32)).astype(x.dtype)


def layernorm(x, g, b):
    xf = x.astype(jnp.float32)
    mu = jnp.mean(xf, axis=-1, keepdims=True)
    xc = xf - mu
    var = jnp.mean(xc * xc, axis=-1, keepdims=True)
    return (xc * lax.rsqrt(var + EPS) * g + b).astype(x.dtype)


def causal_dwconv(x, buf, w, b):
    xp = jnp.concatenate([buf.astype(x.dtype), x], axis=1)
    c = x.shape[-1]
    y = lax.conv_general_dilated(xp, w.astype(xp.dtype)[:, None, :], window_strides=(1,), padding='VALID',
                                 dimension_numbers=('NWC', 'WIO', 'NWC'), feature_group_count=c)
    return y + b, xp[:, -(w.shape[0] - 1):]


def ssd_chunked(x, dt, a, bm, cm, h0):
    b, t, g, e, p = x.shape
    n = bm.shape[-1]
    nc = t // SSD_CHUNK
    L = SSD_CHUNK
    xs = (x * dt[..., None]).reshape(b, nc, L, g, e, p)
    a_cs = jnp.cumsum((dt * a).reshape(b, nc, L, g, e), axis=2)
    bc = bm.reshape(b, nc, L, g, n)
    cc = cm.reshape(b, nc, L, g, n)
    seg = a_cs[:, :, :, None] - a_cs[:, :, None, :]
    causal = jnp.tril(jnp.ones((L, L), dtype=bool))[:, :, None, None]
    decay = jnp.exp(jnp.where(causal, seg, -jnp.inf))
    cb = jnp.einsum('bclgn,bcsgn->bclsg', cc, bc)
    y_diag = jnp.einsum('bclsge,bcsgep->bclgep', cb[..., None] * decay, xs)
    decay_to_end = jnp.exp(a_cs[:, :, -1:] - a_cs)
    states = jnp.einsum('bclgn,bclgep->bcgepn', bc, xs * decay_to_end[..., None])
    chunk_decay = jnp.exp(a_cs[:, :, -1])

    def step(h_prev, inp):
        s_c, d_c = inp
        return d_c[..., None, None] * h_prev + s_c, h_prev

    h_last, h_starts = lax.scan(step, h0, (jnp.moveaxis(states, 1, 0), jnp.moveaxis(chunk_decay, 1, 0)))
    h_starts = jnp.moveaxis(h_starts, 0, 1)
    y_off = jnp.einsum('bclgn,bcgepn->bclgep', cc, h_starts) * jnp.exp(a_cs)[..., None]
    return (y_diag + y_off).reshape(b, t, g, e, p), h_last


def ssd_recurrent(x, dt, a, bm, cm, h0):
    def step(h, inp):
        x_t, dt_t, b_t, c_t = inp
        h = jnp.exp(dt_t * a)[..., None, None] * h + jnp.einsum('bge,bgep,bgn->bgepn', dt_t, x_t, b_t)
        return h, jnp.einsum('bgepn,bgn->bgep', h, c_t)

    h_last, ys = lax.scan(step, h0, (jnp.moveaxis(x, 1, 0), jnp.moveaxis(dt, 1, 0),
                                     jnp.moveaxis(bm, 1, 0), jnp.moveaxis(cm, 1, 0)))
    return jnp.moveaxis(ys, 0, 1), h_last


def _sb_block(q, k, v, q_pos, k_pos, bias):
    z = jnp.einsum('bqhd,bshd->bhqs', q, k).astype(jnp.float32) * (ATT_HEAD_DIM ** -0.5)
    z = z + bias.astype(jnp.float32)[None, :, None, None]
    causal = k_pos[None, :] < q_pos[:, None]
    log_keep = jnp.where(causal, jax.nn.log_sigmoid(-z), 0.0)
    later = lax.cumsum(log_keep, axis=3, reverse=True) - log_keep
    w = jnp.where(causal, jnp.exp(jax.nn.log_sigmoid(z) + later), 0.0)
    return jnp.einsum('bhqs,bshd->bqhd', w, v.astype(jnp.float32))


def stick_breaking(q, k, v, q_pos, k_pos, bias):
    b, t, hh, dh = q.shape
    if t <= Q_BLOCK or t % Q_BLOCK != 0:
        return _sb_block(q, k, v, q_pos, k_pos, bias)
    nb = t // Q_BLOCK
    qb = jnp.moveaxis(q.reshape(b, nb, Q_BLOCK, hh, dh), 1, 0)
    pb = q_pos.reshape(nb, Q_BLOCK)
    ob = lax.map(lambda qp: _sb_block(qp[0], k, v, qp[1], k_pos, bias), (qb, pb))
    return jnp.moveaxis(ob, 0, 1).reshape(b, t, hh, dh)


def mixer(h, p, k_past, v_past, ssd_buf, ssm_h0, conf_buf, prompt):
    b, t, _ = h.shape
    proj = h @ p['w_in']
    z, xbc, dt_raw, glu_a, glu_b, g_conf, q, k, v, g_att = jnp.split(
        proj, np.cumsum(IN_WIDTHS)[:-1].tolist(), axis=-1)
    xbc, new_ssd_buf = causal_dwconv(xbc, ssd_buf, p['ssd_conv_w'], p['ssd_conv_b'])
    xbc = jax.nn.silu(xbc).astype(jnp.float32)
    xs, bm, cm = jnp.split(xbc, [D_SSD, D_SSD + SSD_GROUPS * SSD_STATE], axis=-1)
    xs = xs.reshape(b, t, SSD_GROUPS, SSD_HEADS_PER_GROUP, SSD_HEAD_DIM)
    bm = bm.reshape(b, t, SSD_GROUPS, SSD_STATE)
    cm = cm.reshape(b, t, SSD_GROUPS, SSD_STATE)
    dt = jax.nn.softplus(dt_raw.astype(jnp.float32) + p['ssd_dt_bias'].astype(jnp.float32))
    dt = dt.reshape(b, t, SSD_GROUPS, SSD_HEADS_PER_GROUP)
    a = -jnp.exp(p['ssd_a_log'].astype(jnp.float32)).reshape(SSD_GROUPS, SSD_HEADS_PER_GROUP)
    h0 = ssm_h0.astype(jnp.float32).reshape(b, SSD_GROUPS, SSD_HEADS_PER_GROUP, SSD_HEAD_DIM, SSD_STATE)
    scan_fn = ssd_chunked if prompt else ssd_recurrent
    y, h_last = scan_fn(xs, dt, a, bm, cm, h0)
    y = y + p['ssd_d'].astype(jnp.float32).reshape(SSD_GROUPS, SSD_HEADS_PER_GROUP, 1) * xs
    y = y.reshape(b, t, D_SSD) * jax.nn.silu(z.astype(jnp.float32))
    y = _rms(y.reshape(b, t, SSD_GROUPS, D_SSD // SSD_GROUPS)).reshape(b, t, D_SSD) * p['ssd_norm_g']
    y_ssd = y.astype(h.dtype)
    u = glu_a * jax.nn.sigmoid(glu_b)
    u, new_conf_buf = causal_dwconv(u, conf_buf, p['conf_conv_w'], p['conf_conv_b'])
    u = jax.nn.silu(layernorm(u, p['conf_ln_g'], p['conf_ln_b']))
    y_conf = ((u @ p['conf_w_pw']) * jax.nn.silu(g_conf)).astype(h.dtype)
    q = q.reshape(b, t, N_ATT_HEADS, ATT_HEAD_DIM)
    k = k.reshape(b, t, N_ATT_HEADS, ATT_HEAD_DIM)
    v = v.reshape(b, t, N_ATT_HEADS, ATT_HEAD_DIM)
    if k_past is None:
        k_all, v_all, past = k, v, 0
    else:
        k_all = jnp.concatenate([k_past.astype(k.dtype), k], axis=1)
        v_all = jnp.concatenate([v_past.astype(v.dtype), v], axis=1)
        past = k_past.shape[1]
    q_pos = past + jnp.arange(t)
    k_pos = jnp.arange(past + t)
    o = stick_breaking(q, k_all, v_all, q_pos, k_pos, p['att_logit_bias']).reshape(b, t, D_ATT)
    y_att = (o * jax.nn.silu(g_att.astype(jnp.float32))).astype(h.dtype)
    out = jnp.concatenate([y_ssd, y_conf, y_att], axis=-1) @ p['w_out']
    new_h = h_last.reshape(b, N_SSD_HEADS, SSD_HEAD_DIM, SSD_STATE)
    return out, (k, v, new_h, new_ssd_buf, new_conf_buf)


def trunk(x, c, layers, pasts, final_norm_g, prompt):
    new_states = []
    for l in range(DEPTH):
        p = layers[l]
        mod = jax.nn.silu(c) @ p['w_ada'] + p['b_ada']
        shift, scale, gate = jnp.split(mod[:, None, :], 3, axis=-1)
        h = (rmsnorm(x, p['norm_g']) * (1 + scale) + shift).astype(x.dtype)
        out, st = mixer(h, p, *pasts[l], prompt)
        x = (x + gate * out).astype(x.dtype)
        new_states.append(st)
    return rmsnorm(x, final_norm_g), new_states


def _stack(states, i):
    return jnp.stack([s[i] for s in states])


def setup_inputs(seed: int = 0) -> dict:
    key = jax.random.key(seed)
    ks = jax.random.split(key, 28)
    f32 = jnp.float32
    n_pages = PAST_LEN // PAGE_SIZE
    n_pool = (DEC_BATCH * n_pages * 5) // 4

    def nrm(k, shape, s):
        return jax.random.normal(k, shape, f32) * s

    dt0 = jnp.exp(jax.random.uniform(ks[16], (DEPTH, N_SSD_HEADS), f32, math.log(DT_MIN), math.log(DT_MAX)))
    page_table = jax.random.permutation(ks[6], n_pool)[: DEC_BATCH * n_pages]
    return {
        'x_prompt': nrm(ks[0], (BATCH, SEQ, D_MODEL), 1.0),
        'x_sample': nrm(ks[1], (DEC_BATCH, DEC_SEQ, D_MODEL), 1.0),
        'c_prompt': nrm(ks[2], (BATCH, D_MODEL), 1.0),
        'c_sample': nrm(ks[3], (DEC_BATCH, D_MODEL), 1.0),
        'cache_k': nrm(ks[4], (DEPTH, n_pool, PAGE_SIZE, N_ATT_HEADS, ATT_HEAD_DIM), 1.0),
        'cache_v': nrm(ks[5], (DEPTH, n_pool, PAGE_SIZE, N_ATT_HEADS, ATT_HEAD_DIM), 1.0),
        'page_table': page_table.reshape(DEC_BATCH, n_pages).astype(jnp.int32),
        'state_ssm': nrm(ks[7], (DEPTH, DEC_BATCH, N_SSD_HEADS, SSD_HEAD_DIM, SSD_STATE), 0.1),
        'state_ssd_conv': nrm(ks[8], (DEPTH, DEC_BATCH, SSD_CONV - 1, SSD_CONV_DIM), 1.0),
        'state_conf_conv': nrm(ks[9], (DEPTH, DEC_BATCH, CONF_WIDTH - 1, D_CONF), 0.5),
        'norm_g': 1.0 + nrm(ks[10], (DEPTH, D_MODEL), 0.01),
        'w_ada': nrm(ks[11], (DEPTH, D_MODEL, 3 * D_MODEL), 0.5 * D_MODEL ** -0.5),
        'b_ada': nrm(ks[12], (DEPTH, 3 * D_MODEL), 0.01),
        'w_in': nrm(ks[13], (DEPTH, D_MODEL, D_IN_PROJ), D_MODEL ** -0.5),
        'ssd_conv_w': nrm(ks[14], (DEPTH, SSD_CONV, SSD_CONV_DIM), SSD_CONV ** -0.5),
        'ssd_conv_b': nrm(ks[15], (DEPTH, SSD_CONV_DIM), 0.01),
        'ssd_dt_bias': dt0 + jnp.log(-jnp.expm1(-dt0)),
        'ssd_a_log': jnp.log(jax.random.uniform(ks[17], (DEPTH, N_SSD_HEADS), f32, 1.0, 16.0)),
        'ssd_d': 1.0 + nrm(ks[18], (DEPTH, N_SSD_HEADS), 0.1),
        'ssd_norm_g': 1.0 + nrm(ks[19], (DEPTH, D_SSD), 0.01),
        'conf_conv_w': nrm(ks[20], (DEPTH, CONF_WIDTH, D_CONF), CONF_WIDTH ** -0.5),
        'conf_conv_b': nrm(ks[21], (DEPTH, D_CONF), 0.01),
        'conf_ln_g': 1.0 + nrm(ks[22], (DEPTH, D_CONF), 0.01),
        'conf_ln_b': nrm(ks[23], (DEPTH, D_CONF), 0.01),
        'conf_w_pw': nrm(ks[24], (DEPTH, D_CONF, D_CONF), D_CONF ** -0.5),
        'att_logit_bias': ATT_BIAS_INIT + nrm(ks[27], (DEPTH, N_ATT_HEADS), 0.1),
        'w_out': nrm(ks[25], (DEPTH, D_MIX, D_MODEL), D_MIX ** -0.5),
        'final_norm_g': 1.0 + nrm(ks[26], (D_MODEL,), 0.01),
    }


def reference(x_prompt, x_sample, c_prompt, c_sample, cache_k, cache_v, page_table, state_ssm,
              state_ssd_conv, state_conf_conv, norm_g, w_ada, b_ada, w_in, ssd_conv_w, ssd_conv_b,
              ssd_dt_bias, ssd_a_log, ssd_d, ssd_norm_g, conf_conv_w, conf_conv_b, conf_ln_g,
              conf_ln_b, conf_w_pw, att_logit_bias, w_out, final_norm_g):
    layers = [dict(norm_g=norm_g[l], w_ada=w_ada[l], b_ada=b_ada[l], w_in=w_in[l],
                   ssd_conv_w=ssd_conv_w[l], ssd_conv_b=ssd_conv_b[l], ssd_dt_bias=ssd_dt_bias[l],
                   ssd_a_log=ssd_a_log[l], ssd_d=ssd_d[l], ssd_norm_g=ssd_norm_g[l],
                   conf_conv_w=conf_conv_w[l], conf_conv_b=conf_conv_b[l], conf_ln_g=conf_ln_g[l],
                   conf_ln_b=conf_ln_b[l], conf_w_pw=conf_w_pw[l], att_logit_bias=att_logit_bias[l],
                   w_out=w_out[l])
              for l in range(DEPTH)]
    bp = x_prompt.shape[0]
    bs, n_pages = page_table.shape
    past_rows = n_pages * PAGE_SIZE
    pasts_prompt = [(None, None,
                     jnp.zeros((bp, SSD_CONV - 1, SSD_CONV_DIM), x_prompt.dtype),
                     jnp.zeros((bp, N_SSD_HEADS, SSD_HEAD_DIM, SSD_STATE), jnp.float32),
                     jnp.zeros((bp, CONF_WIDTH - 1, D_CONF), x_prompt.dtype))
                    for _ in range(DEPTH)]
    pasts_sample = [(cache_k[l][page_table].reshape(bs, past_rows, N_ATT_HEADS, ATT_HEAD_DIM),
                     cache_v[l][page_table].reshape(bs, past_rows, N_ATT_HEADS, ATT_HEAD_DIM),
                     state_ssd_conv[l], state_ssm[l], state_conf_conv[l])
                    for l in range(DEPTH)]
    y_prompt, st_p = trunk(x_prompt, c_prompt, layers, pasts_prompt, final_norm_g, True)
    y_sample, st_s = trunk(x_sample, c_sample, layers, pasts_sample, final_norm_g, False)
    return (y_prompt, y_sample,
            _stack(st_p, 0), _stack(st_p, 1), _stack(st_s, 0), _stack(st_s, 1),
            _stack(st_p, 2), _stack(st_s, 2), _stack(st_p, 3), _stack(st_s, 3),
            _stack(st_p, 4), _stack(st_s, 4))
```

```python
import functools

import jax
import jax.numpy as jnp
from jax import lax
from jax.experimental import pallas as pl
from jax.experimental.pallas import tpu as pltpu

F32 = jnp.float32
BF16 = jnp.bfloat16
EPS = 1e-6

D_MODEL = 1024
D_SSD = 1024
SSD_HEAD_DIM = 64
N_SSD_HEADS = 16
SSD_GROUPS = 2
SSD_STATE = 128
SSD_CONV = 4
SSD_CONV_DIM = D_SSD + 2 * SSD_GROUPS * SSD_STATE
D_CONF = 512
CONF_WIDTH = 31
D_ATT = 512
N_ATT_HEADS = 8
ATT_HEAD_DIM = 64
PAGE_SIZE = 128

LANES = 128
SSD_CHUNK = 128
DT_PAD = LANES
PROJ_COLS = (("z", D_SSD), ("xbc", SSD_CONV_DIM), ("glu", 3 * D_CONF), ("q", D_ATT), ("k", D_ATT),
             ("v", D_ATT), ("ga", D_ATT), ("dt", DT_PAD))
D_PROJ = sum(w for _, w in PROJ_COLS)
VMEM_LIMIT = 56 * 1024 * 1024


def _cparams(*sem):
    return pltpu.CompilerParams(dimension_semantics=sem, vmem_limit_bytes=VMEM_LIMIT)


def _silu(x):
    return x * jax.nn.sigmoid(x)


def _softplus(x):
    return jnp.maximum(x, 0.0) + jnp.log1p(jnp.exp(-jnp.abs(x)))


def _split3(v):
    hi = v.astype(BF16)
    r = v - hi.astype(F32)
    mid = r.astype(BF16)
    lo = (r - mid.astype(F32)).astype(BF16)
    return hi, mid, lo


def _dot(a, b):
    return jnp.dot(a, b, preferred_element_type=F32)


def _dot_nt(a, b):
    return lax.dot_general(a, b, (((1,), (1,)), ((), ())), preferred_element_type=F32)


def _dot_exact_rhs(v, m):
    hi, mid, lo = _split3(v)
    return _dot(hi, m) + _dot(mid, m) + _dot(lo, m)


def _dot_exact_lhs(m, v):
    hi, mid, lo = _split3(v)
    return _dot(m, hi) + _dot(m, mid) + _dot(m, lo)


def _ada_kernel(c_ref, w_ref, b_ref, o_ref):
    s = _silu(c_ref[...]).astype(BF16)
    o_ref[...] = _dot(s, w_ref[...].astype(BF16)) + b_ref[...]


def _ada_call(c_all, w_ada, b_ada):
    depth, d, d3 = w_ada.shape
    rows = c_all.shape[0]
    tn = 1024
    return pl.pallas_call(
        _ada_kernel,
        grid=(depth, d3 // tn),
        in_specs=[pl.BlockSpec((rows, d), lambda l, j: (0, 0)),
                  pl.BlockSpec((None, d, tn), lambda l, j: (l, 0, j)),
                  pl.BlockSpec((None, 1, tn), lambda l, j: (l, 0, j))],
        out_specs=pl.BlockSpec((None, rows, tn), lambda l, j: (l, 0, j)),
        out_shape=jax.ShapeDtypeStruct((depth, rows, d3), F32),
        compiler_params=_cparams("arbitrary", "arbitrary"),
        name="ada_mod",
    )(c_all, w_ada, b_ada.reshape(depth, 1, d3))


def _inproj_kernel(x_ref, shift_ref, scale_ref, g_ref, wt_ref, *out_refs, kv_transposed):
    x = x_ref[...]
    xn = x * lax.rsqrt(jnp.mean(x * x, axis=-1, keepdims=True) + EPS) * g_ref[...]
    h = (xn * (1.0 + scale_ref[...]) + shift_ref[...]).astype(BF16)
    lo = 0
    for (name, width), o_ref in zip(PROJ_COLS, out_refs):
        if kv_transposed and name in ("k", "v"):
            o_ref[...] = _dot_nt(wt_ref[lo:lo + width, :], h)
        else:
            o_ref[...] = _dot_nt(h, wt_ref[lo:lo + width, :])
        lo += width


def _inproj_call(x, mod, norm_g, wt_bf16, *, tm, rows_per_batch):
    n, d = x.shape
    kv_transposed = rows_per_batch > 1
    out_specs = [pl.BlockSpec((tm, w), lambda i: (i, 0)) for _, w in PROJ_COLS]
    out_shape = [jax.ShapeDtypeStruct((n, w), F32) for _, w in PROJ_COLS]
    if kv_transposed:
        per = rows_per_batch // tm
        shift_spec = pl.BlockSpec((None, 1, d), lambda i: (i // per, 0, 0))
        scale_spec = pl.BlockSpec((None, 1, d), lambda i: (i // per, 0, 1))
        for idx, (name, w) in enumerate(PROJ_COLS):
            if name in ("k", "v"):
                out_specs[idx] = pl.BlockSpec((None, w, tm), lambda i: (i // per, 0, i % per))
                out_shape[idx] = jax.ShapeDtypeStruct((n // rows_per_batch, w, rows_per_batch), F32)
    else:
        shift_spec = pl.BlockSpec((tm, d), lambda i: (i, 0))
        scale_spec = pl.BlockSpec((tm, d), lambda i: (i, 1))
    return pl.pallas_call(
        functools.partial(_inproj_kernel, kv_transposed=kv_transposed),
        grid=(n // tm,),
        in_specs=[pl.BlockSpec((tm, d), lambda i: (i, 0)), shift_spec, scale_spec,
                  pl.BlockSpec((1, d), lambda i: (0, 0)),
                  pl.BlockSpec((D_PROJ, d), lambda i: (0, 0), pipeline_mode=pl.Buffered(1))],
        out_specs=out_specs,
        out_shape=out_shape,
        compiler_params=_cparams("arbitrary"),
        name="norm_inproj",
    )(x, mod, mod, norm_g.reshape(1, d), wt_bf16)


def _ssd_kernel(xbc_ref, dt_ref, z_ref, cw_ref, cb_ref, dtb_ref, alog_ref, dexp_ref, ng_ref,
                y_ref, hout_ref, convout_ref, xp_ref, ht_ref, yacc_ref):
    c = pl.program_id(1)
    L = SSD_CHUNK
    tail = SSD_CONV - 1
    base = 8 - tail

    @pl.when(c == 0)
    def _():
        ht_ref[...] = jnp.zeros_like(ht_ref)
        xp_ref[0:8, :] = jnp.zeros((8, SSD_CONV_DIM), F32)

    xp_ref[8:8 + L, :] = xbc_ref[...]
    acc = jnp.broadcast_to(cb_ref[...], (L, SSD_CONV_DIM))
    for k in range(SSD_CONV):
        acc = acc + cw_ref[k:k + 1, :] * xp_ref[base + k:base + k + L, :]
    new_tail = xp_ref[8 + L - tail:8 + L, :]
    convout_ref[...] = new_tail
    xp_ref[base:8, :] = new_tail
    xa = _silu(acc)
    xs = xa[:, :D_SSD]

    dt = _softplus(dt_ref[...] + dtb_ref[...])
    a = -jnp.exp(alog_ref[...])
    da = dt * a
    row = lax.broadcasted_iota(jnp.int32, (L, L), 0)
    col = lax.broadcasted_iota(jnp.int32, (L, L), 1)
    causal = col <= row
    tril = causal.astype(BF16)
    a_cs = _dot_exact_lhs(tril, da)
    a_cs_t = a_cs.T
    dt_t = dt.T
    a_last = a_cs[L - 1:L, :]
    wgt = dt * jnp.exp(a_last - a_cs)
    expa = jnp.exp(a_cs)
    cdec = jnp.exp(a_last)

    hpg = N_SSD_HEADS // SSD_GROUPS
    for g in range(SSD_GROUPS):
        bg = xa[:, D_SSD + g * SSD_STATE:D_SSD + (g + 1) * SSD_STATE]
        cg = xa[:, D_SSD + (SSD_GROUPS + g) * SSD_STATE:D_SSD + (SSD_GROUPS + g + 1) * SSD_STATE]
        bgb = bg.astype(BF16)
        cgb = cg.astype(BF16)
        bgt = bg.T.astype(BF16)
        cbm = _dot_nt(cgb, bgb)
        for e8 in range(hpg):
            e = g * hpg + e8
            sl = slice(e * SSD_HEAD_DIM, (e + 1) * SSD_HEAD_DIM)
            seg = a_cs[:, e:e + 1] - a_cs_t[e:e + 1, :]
            dec = jnp.exp(jnp.where(causal, seg, -jnp.inf))
            m = (cbm * dec * dt_t[e:e + 1, :]).astype(BF16)
            xe = xs[:, sl]
            y_diag = _dot(m, xe.astype(BF16))
            hte = ht_ref[:, sl]
            y_off = _dot(cgb, hte.astype(BF16)) * expa[:, e:e + 1]
            xw = (xe * wgt[:, e:e + 1]).astype(BF16)
            ht_ref[:, sl] = cdec[:, e:e + 1] * hte + _dot(bgt, xw)
            yacc_ref[:, sl] = y_diag + y_off

    y = yacc_ref[...] + dexp_ref[...] * xs
    y = y * _silu(z_ref[...])
    gw = D_SSD // SSD_GROUPS
    for g in range(SSD_GROUPS):
        yg = y[:, g * gw:(g + 1) * gw]
        yg = yg * lax.rsqrt(jnp.mean(yg * yg, axis=-1, keepdims=True) + EPS)
        y_ref[:, g * gw:(g + 1) * gw] = yg * ng_ref[:, g * gw:(g + 1) * gw]

    @pl.when(c == pl.num_programs(1) - 1)
    def _():
        hout_ref[...] = ht_ref[...].T


def _ssd_call(xbc, dt, z, p, *, nb, t):
    nc = t // SSD_CHUNK
    L = SSD_CHUNK
    row = lambda b, c: (b * nc + c, 0)
    const = lambda b, c: (0, 0)
    return pl.pallas_call(
        _ssd_kernel,
        grid=(nb, nc),
        in_specs=[pl.BlockSpec((L, SSD_CONV_DIM), row), pl.BlockSpec((L, DT_PAD), row),
                  pl.BlockSpec((L, D_SSD), row),
                  pl.BlockSpec((SSD_CONV, SSD_CONV_DIM), const), pl.BlockSpec((1, SSD_CONV_DIM), const),
                  pl.BlockSpec((1, DT_PAD), const), pl.BlockSpec((1, DT_PAD), const),
                  pl.BlockSpec((1, D_SSD), const), pl.BlockSpec((1, D_SSD), const)],
        out_specs=[pl.BlockSpec((L, D_SSD), row),
                   pl.BlockSpec((None, D_SSD, SSD_STATE), lambda b, c: (b, 0, 0)),
                   pl.BlockSpec((None, SSD_CONV - 1, SSD_CONV_DIM), lambda b, c: (b, 0, 0))],
        out_shape=[jax.ShapeDtypeStruct((nb * t, D_SSD), F32),
                   jax.ShapeDtypeStruct((nb, D_SSD, SSD_STATE), F32),
                   jax.ShapeDtypeStruct((nb, SSD_CONV - 1, SSD_CONV_DIM), F32)],
        scratch_shapes=[pltpu.VMEM((8 + L, SSD_CONV_DIM), F32),
                        pltpu.VMEM((SSD_STATE, D_SSD), F32),
                        pltpu.VMEM((L, D_SSD), F32)],
        compiler_params=_cparams("arbitrary", "arbitrary"),
        name="ssd_prompt",
    )(xbc, dt, z, p["ssd_conv_w"], p["ssd_conv_b"], p["dtb_pad"], p["alog_pad"], p["d_exp"], p["ssd_norm_g"])


def _ssd_step_kernel(xbc_ref, dt_ref, z_ref, conv_ref, h_ref, cw_ref, cb_ref, dtb_ref, alog_ref, dexp_ref,
                     ng_ref, e_ref, y_ref, hout_ref, convout_ref, r_ref):
    x = xbc_ref[...]
    buf = conv_ref[...]
    tail = SSD_CONV - 1
    acc = cb_ref[...] + cw_ref[tail:tail + 1, :] * x
    for k in range(tail):
        acc = acc + cw_ref[k:k + 1, :] * buf[k:k + 1, :]
    convout_ref[0:tail - 1, :] = buf[1:tail, :]
    convout_ref[tail - 1:tail, :] = x
    xa = _silu(acc)
    xs = xa[:, :D_SSD]

    dt = _softplus(dt_ref[...] + dtb_ref[...])
    dt_e = _dot_exact_rhs(jnp.broadcast_to(dt, (8, DT_PAD)), e_ref[...])[0:1, :]
    d_a = jnp.exp(dt_e * (-jnp.exp(alog_ref[...])))
    xdt = xs * dt_e

    @pl.when(pl.program_id(0) == 0)
    def _():
        r_ref[...] = jnp.zeros_like(r_ref)

    r_ref[0:1, :] = xdt
    r_ref[1:2, :] = d_a
    rt = r_ref[...].T
    gw = D_SSD // SSD_GROUPS
    for g in range(SSD_GROUPS):
        rows = slice(g * gw, (g + 1) * gw)
        bg = xa[:, D_SSD + g * SSD_STATE:D_SSD + (g + 1) * SSD_STATE]
        cg = xa[:, D_SSD + (SSD_GROUPS + g) * SSD_STATE:D_SSD + (SSD_GROUPS + g + 1) * SSD_STATE]
        hn = rt[rows, 1:2] * h_ref[rows, :] + rt[rows, 0:1] * bg
        hout_ref[rows, :] = hn
        c8 = jnp.broadcast_to(cg, (8, SSD_STATE)).astype(BF16)
        yg = _dot_nt(c8, hn.astype(BF16))[0:1, :]
        yg = yg + dexp_ref[:, rows] * xs[:, rows]
        yg = yg * _silu(z_ref[:, rows])
        yg = yg * lax.rsqrt(jnp.mean(yg * yg, axis=-1, keepdims=True) + EPS)
        y_ref[:, rows] = yg * ng_ref[:, rows]


def _ssd_step_call(xbc, dt, z, conv_state, ssm_state, p, expand):
    bs = xbc.shape[0]
    per_b = lambda b: (b, 0, 0)
    const = lambda b: (0, 0)
    tail = SSD_CONV - 1
    y, h, cv = pl.pallas_call(
        _ssd_step_kernel,
        grid=(bs,),
        in_specs=[pl.BlockSpec((None, 1, SSD_CONV_DIM), per_b), pl.BlockSpec((None, 1, DT_PAD), per_b),
                  pl.BlockSpec((None, 1, D_SSD), per_b), pl.BlockSpec((None, tail, SSD_CONV_DIM), per_b),
                  pl.BlockSpec((None, D_SSD, SSD_STATE), per_b),
                  pl.BlockSpec((SSD_CONV, SSD_CONV_DIM), const), pl.BlockSpec((1, SSD_CONV_DIM), const),
                  pl.BlockSpec((1, DT_PAD), const), pl.BlockSpec((1, D_SSD), const),
                  pl.BlockSpec((1, D_SSD), const), pl.BlockSpec((1, D_SSD), const),
                  pl.BlockSpec((DT_PAD, D_SSD), const)],
        out_specs=[pl.BlockSpec((None, 1, D_SSD), per_b), pl.BlockSpec((None, D_SSD, SSD_STATE), per_b),
                   pl.BlockSpec((None, tail, SSD_CONV_DIM), per_b)],
        out_shape=[jax.ShapeDtypeStruct((bs, 1, D_SSD), F32),
                   jax.ShapeDtypeStruct((bs, D_SSD, SSD_STATE), F32),
                   jax.ShapeDtypeStruct((bs, tail, SSD_CONV_DIM), F32)],
        scratch_shapes=[pltpu.VMEM((LANES, D_SSD), F32)],
        compiler_params=_cparams("arbitrary"),
        name="ssd_step",
    )(xbc.reshape(bs, 1, SSD_CONV_DIM), dt.reshape(bs, 1, DT_PAD), z.reshape(bs, 1, D_SSD), conv_state,
      ssm_state.reshape(bs, D_SSD, SSD_STATE), p["ssd_conv_w"], p["ssd_conv_b"], p["dtb_pad"],
      p["alog_exp"], p["d_exp"], p["ssd_norm_g"], expand)
    return y.reshape(bs, D_SSD), h, cv


def _conf_tail(u, gc, lg_ref, lb_ref, pw_ref):
    mu = jnp.mean(u, axis=-1, keepdims=True)
    uc = u - mu
    var = jnp.mean(uc * uc, axis=-1, keepdims=True)
    un = _silu(uc * lax.rsqrt(var + EPS) * lg_ref[...] + lb_ref[...])
    return _dot(un.astype(BF16), pw_ref[...]) * _silu(gc)


def _conf_kernel(glu_ref, w_ref, b_ref, lg_ref, lb_ref, pw_ref, y_ref, bufout_ref, ub_ref, cv_ref, *, tb, rb):
    c = pl.program_id(1)
    hist = CONF_WIDTH - 1
    pad = 32

    @pl.when(c == 0)
    def _():
        ub_ref[0:pad, :] = jnp.zeros((pad, D_CONF), F32)

    ub_ref[pad:pad + tb, :] = glu_ref[:, 0:D_CONF] * jax.nn.sigmoid(glu_ref[:, D_CONF:2 * D_CONF])
    for r in range(tb // rb):
        acc = jnp.broadcast_to(b_ref[...], (rb, D_CONF))
        for k in range(CONF_WIDTH):
            lo = pad - hist + k + r * rb
            acc = acc + w_ref[k:k + 1, :] * ub_ref[lo:lo + rb, :]
        cv_ref[r * rb:(r + 1) * rb, :] = acc
    new_hist = ub_ref[pad + tb - hist:pad + tb, :]
    bufout_ref[...] = new_hist
    ub_ref[pad - hist:pad, :] = new_hist
    y_ref[...] = _conf_tail(cv_ref[...], glu_ref[:, 2 * D_CONF:3 * D_CONF], lg_ref, lb_ref, pw_ref)


def _conf_call(glu, p, *, nb, t):
    tb = min(256, t)
    nt = t // tb
    hist = CONF_WIDTH - 1
    row = lambda b, c: (b * nt + c, 0)
    const = lambda b, c: (0, 0)
    return pl.pallas_call(
        functools.partial(_conf_kernel, tb=tb, rb=32),
        grid=(nb, nt),
        in_specs=[pl.BlockSpec((tb, 3 * D_CONF), row),
                  pl.BlockSpec((CONF_WIDTH, D_CONF), const), pl.BlockSpec((1, D_CONF), const),
                  pl.BlockSpec((1, D_CONF), const), pl.BlockSpec((1, D_CONF), const),
                  pl.BlockSpec((D_CONF, D_CONF), const)],
        out_specs=[pl.BlockSpec((tb, D_CONF), row),
                   pl.BlockSpec((None, hist, D_CONF), lambda b, c: (b, 0, 0))],
        out_shape=[jax.ShapeDtypeStruct((nb * t, D_CONF), F32),
                   jax.ShapeDtypeStruct((nb, hist, D_CONF), F32)],
        scratch_shapes=[pltpu.VMEM((32 + tb, D_CONF), F32), pltpu.VMEM((tb, D_CONF), F32)],
        compiler_params=_cparams("arbitrary", "arbitrary"),
        name="conformer_prompt",
    )(glu, p["conf_conv_w"], p["conf_conv_b"], p["conf_ln_g"], p["conf_ln_b"], p["pw_bf16"])


def _conf_step_kernel(glu_ref, buf_ref, w_ref, b_ref, lg_ref, lb_ref, pw_ref, y_ref, bufout_ref):
    hist = CONF_WIDTH - 1
    u = glu_ref[:, 0:D_CONF] * jax.nn.sigmoid(glu_ref[:, D_CONF:2 * D_CONF])
    acc = b_ref[...] + w_ref[hist:hist + 1, :] * u
    for k in range(hist):
        acc = acc + w_ref[k:k + 1, :] * buf_ref[k]
    for k in range(hist - 1):
        bufout_ref[k] = buf_ref[k + 1]
    bufout_ref[hist - 1] = u
    y_ref[...] = _conf_tail(acc, glu_ref[:, 2 * D_CONF:3 * D_CONF], lg_ref, lb_ref, pw_ref)


def _conf_step_call(glu, buf_tap_major, p):
    bs = glu.shape[0]
    hist = CONF_WIDTH - 1
    return pl.pallas_call(
        _conf_step_kernel,
        out_shape=[jax.ShapeDtypeStruct((bs, D_CONF), F32),
                   jax.ShapeDtypeStruct((hist, bs, D_CONF), F32)],
        compiler_params=pltpu.CompilerParams(vmem_limit_bytes=VMEM_LIMIT),
        name="conformer_step",
    )(glu, buf_tap_major, p["conf_conv_w"], p["conf_conv_b"], p["conf_ln_g"], p["conf_ln_b"], p["pw_bf16"])


def _att_kernel(bias_ref, q_ref, kt_ref, vt_ref, ga_ref, o_ref, kb_ref, vb_ref, *, tq):
    pair = pl.program_id(1)
    qi = pl.program_id(2)

    @pl.when(qi == 0)
    def _():
        for j in range(kb_ref.shape[0]):
            kb_ref[j] = kt_ref[:, j * tq:(j + 1) * tq].astype(BF16)
            vb_ref[j] = vt_ref[:, j * tq:(j + 1) * tq].astype(BF16)

    q = q_ref[...] * (ATT_HEAD_DIM ** -0.5)
    lane = lax.broadcasted_iota(jnp.int32, (tq, LANES), 1)
    row = lax.broadcasted_iota(jnp.int32, (tq, tq), 0)
    col = lax.broadcasted_iota(jnp.int32, (tq, tq), 1)
    causal = col < row
    later = (row > col).astype(BF16)

    def tile(j, masked, qh, bias, carry):
        acc, rsp = carry
        z = _dot(qh, kb_ref[j]) + bias
        sp = _softplus(z)
        lsig = z - sp
        if masked:
            sp = jnp.where(causal, sp, 0.0)
        csp = _dot(sp.astype(BF16), later)
        w = jnp.exp(lsig - csp - rsp)
        if masked:
            w = jnp.where(causal, w, 0.0)
        acc = acc + _dot_nt(w.astype(BF16), vb_ref[j])
        rsp = rsp + jnp.sum(sp, axis=-1, keepdims=True)
        return acc, rsp

    accs = []
    for hh in range(2):
        bias = bias_ref[pair * 2 + hh]
        qh = jnp.where(lane // ATT_HEAD_DIM == hh, q, 0.0).astype(BF16)
        carry = (jnp.zeros((tq, LANES), F32), jnp.zeros((tq, 1), F32))
        carry = tile(qi, True, qh, bias, carry)
        carry = lax.fori_loop(0, qi, lambda it, cr: tile(qi - 1 - it, False, qh, bias, cr), carry)
        accs.append(carry[0])
    o = jnp.where(lane < ATT_HEAD_DIM, accs[0], accs[1])
    o_ref[...] = o * _silu(ga_ref[...])


def _att_call(q, kt, vt, ga, bias, *, nb, t):
    tq = min(256, t)
    nq = t // tq
    npair = D_ATT // LANES
    qspec = pl.BlockSpec((tq, LANES), lambda b, pr, i: (b * nq + i, pr))
    kspec = pl.BlockSpec((None, LANES, t), lambda b, pr, i: (b, pr, 0))
    return pl.pallas_call(
        functools.partial(_att_kernel, tq=tq),
        grid=(nb, npair, nq),
        in_specs=[pl.BlockSpec(memory_space=pltpu.SMEM), qspec, kspec, kspec, qspec],
        out_specs=qspec,
        out_shape=jax.ShapeDtypeStruct((nb * t, D_ATT), F32),
        scratch_shapes=[pltpu.VMEM((nq, LANES, tq), BF16), pltpu.VMEM((nq, LANES, tq), BF16)],
        compiler_params=_cparams("arbitrary", "arbitrary", "arbitrary"),
        name="stickbreak_prompt",
    )(bias, q, kt, vt, ga)


def _att_dec_kernel(pt_ref, bias_ref, q_ref, ga_ref, *refs, npg):
    del pt_ref
    k_refs = refs[:npg]
    v_refs = refs[npg:2 * npg]
    o_ref, acc_ref, rsp_ref = refs[2 * npg:]
    j = pl.program_id(1)
    nh = N_ATT_HEADS

    @pl.when(j == 0)
    def _():
        acc_ref[...] = jnp.zeros_like(acc_ref)
        rsp_ref[...] = jnp.zeros_like(rsp_ref)

    head = lax.broadcasted_iota(jnp.int32, (nh, D_ATT), 0)
    lane = lax.broadcasted_iota(jnp.int32, (nh, D_ATT), 1)
    own = lane // ATT_HEAD_DIM == head
    q = q_ref[...] * (ATT_HEAD_DIM ** -0.5)
    qbd = jnp.where(own, jnp.broadcast_to(q, (nh, D_ATT)), 0.0).astype(BF16)
    bias = bias_ref[...]
    zs = [_dot(qbd, k_refs[i][...].astype(BF16)) + bias for i in range(npg)]
    sps = [_softplus(z) for z in zs]
    row = lax.broadcasted_iota(jnp.int32, (PAGE_SIZE, PAGE_SIZE), 0)
    col = lax.broadcasted_iota(jnp.int32, (PAGE_SIZE, PAGE_SIZE), 1)
    later = (row > col).astype(BF16)
    loc = _dot_exact_rhs(jnp.concatenate(sps, axis=0), later)
    run = rsp_ref[...]
    acc = acc_ref[...]
    for i in reversed(range(npg)):
        w = jnp.exp(zs[i] - sps[i] - loc[i * nh:(i + 1) * nh, :] - run)
        acc = acc + _dot_nt(w.astype(BF16), v_refs[i][...].astype(BF16))
        run = run + jnp.sum(sps[i], axis=-1, keepdims=True)
    acc_ref[...] = acc
    rsp_ref[...] = run

    @pl.when(j == pl.num_programs(1) - 1)
    def _():
        o = jnp.sum(jnp.where(own, acc, 0.0), axis=0, keepdims=True)
        o_ref[...] = o * _silu(ga_ref[...])


def _att_dec_call(q, ga, cache_k, cache_v, page_table, bias, layer):
    bs, n_pages = page_table.shape
    npg = min(8, n_pages)
    ng = n_pages // npg
    depth, n_pool = cache_k.shape[:2]
    ck = jnp.transpose(cache_k, (0, 1, 3, 4, 2)).reshape(depth, n_pool, D_ATT, PAGE_SIZE)
    cv = jnp.transpose(cache_v, (0, 1, 3, 4, 2)).reshape(depth, n_pool, D_ATT, PAGE_SIZE)

    def page_spec(i):
        return pl.BlockSpec((None, None, D_ATT, PAGE_SIZE),
                            lambda b, j, pt: (layer, pt[b, (ng - 1 - j) * npg + i], 0, 0))

    per_b = lambda b, j, pt: (b, 0, 0)
    grid_spec = pltpu.PrefetchScalarGridSpec(
        num_scalar_prefetch=1,
        grid=(bs, ng),
        in_specs=[pl.BlockSpec((N_ATT_HEADS, 1), lambda b, j, pt: (0, 0)),
                  pl.BlockSpec((None, 1, D_ATT), per_b), pl.BlockSpec((None, 1, D_ATT), per_b)]
                 + [page_spec(i) for i in range(npg)] * 2,
        out_specs=pl.BlockSpec((None, 1, D_ATT), per_b),
        scratch_shapes=[pltpu.VMEM((N_ATT_HEADS, D_ATT), F32), pltpu.VMEM((N_ATT_HEADS, 1), F32)],
    )
    out = pl.pallas_call(
        functools.partial(_att_dec_kernel, npg=npg),
        grid_spec=grid_spec,
        out_shape=jax.ShapeDtypeStruct((bs, 1, D_ATT), F32),
        compiler_params=_cparams("arbitrary", "arbitrary"),
        name="stickbreak_paged",
    )(page_table, bias.reshape(N_ATT_HEADS, 1), q.reshape(bs, 1, D_ATT), ga.reshape(bs, 1, D_ATT),
      *([ck] * npg), *([cv] * npg))
    return out.reshape(bs, D_ATT)


def _outproj_kernel(ys_ref, yc_ref, ya_ref, x_ref, gate_ref, w_ref, fg_ref, o_ref, *, final):
    acc = _dot(ys_ref[...].astype(BF16), w_ref[0:D_SSD, :])
    acc = acc + _dot(yc_ref[...].astype(BF16), w_ref[D_SSD:D_SSD + D_CONF, :])
    acc = acc + _dot(ya_ref[...].astype(BF16), w_ref[D_SSD + D_CONF:, :])
    xn = x_ref[...] + gate_ref[...] * acc
    if final:
        xn = xn * lax.rsqrt(jnp.mean(xn * xn, axis=-1, keepdims=True) + EPS) * fg_ref[...]
    o_ref[...] = xn


def _outproj_call(ys, yc, ya, x, mod, w_bf16, final_g, *, tm, rows_per_batch, final):
    n, d = x.shape
    if rows_per_batch > 1:
        per = rows_per_batch // tm
        gate_spec = pl.BlockSpec((None, 1, d), lambda i: (i // per, 0, 2))
    else:
        gate_spec = pl.BlockSpec((tm, d), lambda i: (i, 2))
    row = lambda i: (i, 0)
    return pl.pallas_call(
        functools.partial(_outproj_kernel, final=final),
        grid=(n // tm,),
        in_specs=[pl.BlockSpec((tm, D_SSD), row), pl.BlockSpec((tm, D_CONF), row), pl.BlockSpec((tm, D_ATT), row),
                  pl.BlockSpec((tm, d), row), gate_spec,
                  pl.BlockSpec((D_SSD + D_CONF + D_ATT, d), lambda i: (0, 0)),
                  pl.BlockSpec((1, d), lambda i: (0, 0))],
        out_specs=pl.BlockSpec((tm, d), row),
        out_shape=jax.ShapeDtypeStruct((n, d), F32),
        compiler_params=_cparams("arbitrary"),
        name="outproj_residual",
    )(ys, yc, ya, x, mod, w_bf16, final_g.reshape(1, d))


def _prep_layer(l, w_in, ssd_conv_w, ssd_conv_b, ssd_dt_bias, ssd_a_log, ssd_d, ssd_norm_g, conf_conv_w,
                conf_conv_b, conf_ln_g, conf_ln_b, conf_w_pw, w_out):
    wt = jnp.swapaxes(w_in, 1, 2)[l]
    dt_lo = D_SSD + SSD_CONV_DIM
    dt_hi = dt_lo + N_SSD_HEADS
    w_r = jnp.concatenate([wt[:dt_lo], wt[dt_hi:],
                           jnp.pad(wt[dt_lo:dt_hi], ((0, DT_PAD - N_SSD_HEADS), (0, 0)))], axis=0)
    pad16 = lambda v: jnp.pad(v, (0, DT_PAD - N_SSD_HEADS)).reshape(1, DT_PAD)
    rep = lambda v: jnp.repeat(v, SSD_HEAD_DIM).reshape(1, D_SSD)
    return dict(
        w_in=w_r.astype(BF16),
        ssd_conv_w=ssd_conv_w[l], ssd_conv_b=ssd_conv_b[l].reshape(1, -1),
        dtb_pad=pad16(ssd_dt_bias[l]), alog_pad=pad16(ssd_a_log[l]), alog_exp=rep(ssd_a_log[l]),
        d_exp=rep(ssd_d[l]), ssd_norm_g=ssd_norm_g[l].reshape(1, -1),
        conf_conv_w=conf_conv_w[l], conf_conv_b=conf_conv_b[l].reshape(1, -1),
        conf_ln_g=conf_ln_g[l].reshape(1, -1), conf_ln_b=conf_ln_b[l].reshape(1, -1),
        pw_bf16=conf_w_pw[l].astype(BF16), w_out=w_out[l].astype(BF16))


def kernel(x_prompt, x_sample, c_prompt, c_sample, cache_k, cache_v, page_table, state_ssm, state_ssd_conv, state_conf_conv, norm_g, w_ada, b_ada, w_in, ssd_conv_w, ssd_conv_b, ssd_dt_bias, ssd_a_log, ssd_d, ssd_norm_g, conf_conv_w, conf_conv_b, conf_ln_g, conf_ln_b, conf_w_pw, att_logit_bias, w_out, final_norm_g):
    nb, t, d = x_prompt.shape
    bs = x_sample.shape[0]
    depth = w_in.shape[0]

    n_mod = nb + bs
    n_mod_pad = -(-n_mod // 8) * 8
    c_all = jnp.concatenate([c_prompt, c_sample, jnp.zeros((n_mod_pad - n_mod, d), F32)], axis=0)
    mod = _ada_call(c_all, w_ada, b_ada)

    lane_head = jnp.arange(D_SSD) // SSD_HEAD_DIM
    expand = (jnp.arange(DT_PAD)[:, None] == lane_head[None, :]).astype(BF16)

    xp = x_prompt.reshape(nb * t, d)
    xs = x_sample.reshape(bs, d)
    tm_p = min(256, t)
    outs_p = []
    outs_s = []
    for l in range(depth):
        p = _prep_layer(l, w_in, ssd_conv_w, ssd_conv_b, ssd_dt_bias, ssd_a_log, ssd_d, ssd_norm_g,
                        conf_conv_w, conf_conv_b, conf_ln_g, conf_ln_b, conf_w_pw, w_out)
        final = l == depth - 1
        mod_p = mod[l, :nb].reshape(nb, 1, 3 * d)
        mod_s = mod[l, nb:nb + bs]

        z, xbc, glu, q, kt, vt, ga, dt = _inproj_call(xp, mod_p, norm_g[l], p["w_in"], tm=tm_p, rows_per_batch=t)
        y_ssd, h_p, cv_p = _ssd_call(xbc, dt, z, p, nb=nb, t=t)
        y_conf, cf_p = _conf_call(glu, p, nb=nb, t=t)
        y_att = _att_call(q, kt, vt, ga, att_logit_bias[l], nb=nb, t=t)
        xp = _outproj_call(y_ssd, y_conf, y_att, xp, mod_p, p["w_out"], final_norm_g, tm=tm_p,
                           rows_per_batch=t, final=final)
        heads_last = lambda a: jnp.transpose(a.reshape(nb, N_ATT_HEADS, ATT_HEAD_DIM, t), (0, 3, 1, 2))
        outs_p.append((heads_last(kt), heads_last(vt),
                       h_p.reshape(nb, N_SSD_HEADS, SSD_HEAD_DIM, SSD_STATE), cv_p, cf_p))

        z, xbc, glu, q, k, v, ga, dt = _inproj_call(xs, mod_s, norm_g[l], p["w_in"], tm=bs, rows_per_batch=1)
        y_ssd, h_s, cv_s = _ssd_step_call(xbc, dt, z, state_ssd_conv[l], state_ssm[l], p, expand)
        y_conf, cf_s = _conf_step_call(glu, jnp.swapaxes(state_conf_conv[l], 0, 1), p)
        y_att = _att_dec_call(q, ga, cache_k, cache_v, page_table, att_logit_bias[l], l)
        xs = _outproj_call(y_ssd, y_conf, y_att, xs, mod_s, p["w_out"], final_norm_g, tm=bs,
                           rows_per_batch=1, final=final)
        outs_s.append((k.reshape(bs, 1, N_ATT_HEADS, ATT_HEAD_DIM), v.reshape(bs, 1, N_ATT_HEADS, ATT_HEAD_DIM),
                       h_s.reshape(bs, N_SSD_HEADS, SSD_HEAD_DIM, SSD_STATE), cv_s, jnp.swapaxes(cf_s, 0, 1)))

    stack = lambda outs, i: jnp.stack([o[i] for o in outs])
    return (xp.reshape(nb, t, d), xs.reshape(bs, 1, d),
            stack(outs_p, 0), stack(outs_p, 1), stack(outs_s, 0), stack(outs_s, 1),
            stack(outs_p, 2), stack(outs_s, 2), stack(outs_p, 3), stack(outs_s, 3),
            stack(outs_p, 4), stack(outs_s, 4))
```

```python
import functools

import jax
import jax.numpy as jnp
from jax import lax
from jax.experimental import pallas as pl
from jax.experimental.pallas import tpu as pltpu

F32 = jnp.float32
BF16 = jnp.bfloat16
EPS = 1e-6
LOG2E = 1.4426950408889634

D_MODEL = 1024
D_SSD = 1024
SSD_HEAD_DIM = 64
N_SSD_HEADS = 16
SSD_GROUPS = 2
SSD_STATE = 128
SSD_CONV = 4
SSD_CONV_DIM = D_SSD + 2 * SSD_GROUPS * SSD_STATE
D_CONF = 512
CONF_WIDTH = 31
D_ATT = 512
N_ATT_HEADS = 8
ATT_HEAD_DIM = 64
PAGE_SIZE = 128

LANES = 128
SSD_CHUNK = 128
ATT_TILE = 256
ATT_HEADS_PER_STEP = 4
CONF_PAD = 32
PAGES_PER_STEP = 32
ATT_ROW_BLOCK = 128
DT_PAD = LANES
PROJ_COLS = (("z", D_SSD), ("xbc", SSD_CONV_DIM), ("glu", 3 * D_CONF), ("q", D_ATT), ("k", D_ATT),
             ("v", D_ATT), ("ga", D_ATT), ("dt", DT_PAD))
D_PROJ = sum(w for _, w in PROJ_COLS)
VMEM_LIMIT = 56 * 1024 * 1024


def _cparams(*sem):
    return pltpu.CompilerParams(dimension_semantics=sem, vmem_limit_bytes=VMEM_LIMIT)


def _silu(x):
    return x * jax.nn.sigmoid(x)


def _softplus(x):
    return jnp.maximum(x, 0.0) + jnp.log1p(jnp.exp(-jnp.abs(x)))


def _split3(v):
    hi = v.astype(BF16)
    r = v - hi.astype(F32)
    mid = r.astype(BF16)
    lo = (r - mid.astype(F32)).astype(BF16)
    return hi, mid, lo


def _dot(a, b):
    return jnp.dot(a, b, preferred_element_type=F32)


def _dot_nt(a, b):
    return lax.dot_general(a, b, (((1,), (1,)), ((), ())), preferred_element_type=F32)


def _dot_exact_rhs(v, m):
    hi, mid, lo = _split3(v)
    return _dot(hi, m) + _dot(mid, m) + _dot(lo, m)


def _dot_exact_lhs(m, v):
    hi, mid, lo = _split3(v)
    return _dot(m, hi) + _dot(m, mid) + _dot(m, lo)


def _ada_kernel(c_ref, w_ref, b_ref, o_ref):
    s = _silu(c_ref[...]).astype(BF16)
    o_ref[...] = _dot(s, w_ref[...].astype(BF16)) + b_ref[...]


def _ada_call(c_all, w_ada, b_ada):
    depth, d, d3 = w_ada.shape
    rows = c_all.shape[0]
    tn = 1024
    return pl.pallas_call(
        _ada_kernel,
        grid=(depth, d3 // tn),
        in_specs=[pl.BlockSpec((rows, d), lambda l, j: (0, 0)),
                  pl.BlockSpec((None, d, tn), lambda l, j: (l, 0, j)),
                  pl.BlockSpec((None, 1, tn), lambda l, j: (l, 0, j))],
        out_specs=pl.BlockSpec((None, rows, tn), lambda l, j: (l, 0, j)),
        out_shape=jax.ShapeDtypeStruct((depth, rows, d3), F32),
        compiler_params=_cparams("arbitrary", "arbitrary"),
        name="ada_mod",
    )(c_all, w_ada, b_ada.reshape(depth, 1, d3))


def _inproj_kernel(x_ref, shift_ref, scale_ref, g_ref, wt_ref, *out_refs, kv_transposed):
    x = x_ref[...]
    xn = x * lax.rsqrt(jnp.mean(x * x, axis=-1, keepdims=True) + EPS) * g_ref[...]
    h = (xn * (1.0 + scale_ref[...]) + shift_ref[...]).astype(BF16)
    lo = 0
    bf16_refs = dict(zip(("k", "v"), out_refs[len(PROJ_COLS):]))
    for (name, width), o_ref in zip(PROJ_COLS, out_refs):
        if kv_transposed and name in ("k", "v"):
            r = _dot_nt(wt_ref[lo:lo + width, :], h)
            o_ref[...] = r
            bf16_refs[name][...] = r.astype(BF16)
        else:
            o_ref[...] = _dot_nt(h, wt_ref[lo:lo + width, :])
        lo += width


def _inproj_call(x, mod, norm_g, wt_bf16, *, tm, rows_per_batch):
    n, d = x.shape
    kv_transposed = rows_per_batch > 1
    out_specs = [pl.BlockSpec((tm, w), lambda i: (i, 0)) for _, w in PROJ_COLS]
    out_shape = [jax.ShapeDtypeStruct((n, w), F32) for _, w in PROJ_COLS]
    if kv_transposed:
        per = rows_per_batch // tm
        shift_spec = pl.BlockSpec((None, 1, d), lambda i: (i // per, 0, 0))
        scale_spec = pl.BlockSpec((None, 1, d), lambda i: (i // per, 0, 1))
        for idx, (name, w) in enumerate(PROJ_COLS):
            if name in ("k", "v"):
                out_specs[idx] = pl.BlockSpec((None, w, tm), lambda i: (i // per, 0, i % per))
                out_shape[idx] = jax.ShapeDtypeStruct((n // rows_per_batch, w, rows_per_batch), F32)
        for _ in ("k", "v"):
            out_specs.append(pl.BlockSpec((None, None, D_ATT, tm), lambda i: (i // per, i % per, 0, 0)))
            out_shape.append(jax.ShapeDtypeStruct((n // rows_per_batch, per, D_ATT, tm), BF16))
    else:
        shift_spec = pl.BlockSpec((tm, d), lambda i: (i, 0))
        scale_spec = pl.BlockSpec((tm, d), lambda i: (i, 1))
    return pl.pallas_call(
        functools.partial(_inproj_kernel, kv_transposed=kv_transposed),
        grid=(n // tm,),
        in_specs=[pl.BlockSpec((tm, d), lambda i: (i, 0)), shift_spec, scale_spec,
                  pl.BlockSpec((1, d), lambda i: (0, 0)),
                  pl.BlockSpec((D_PROJ, d), lambda i: (0, 0), pipeline_mode=pl.Buffered(1))],
        out_specs=out_specs,
        out_shape=out_shape,
        compiler_params=_cparams("arbitrary"),
        name="norm_inproj",
    )(x, mod, mod, norm_g.reshape(1, d), wt_bf16)


def _ssd_kernel(xbc_ref, dt_ref, z_ref, cw_ref, cb_ref, dtb_ref, alog_ref, dexp_ref, ng_ref,
                y_ref, hout_ref, convout_ref, xp_ref, ht_ref, yacc_ref):
    c = pl.program_id(1)
    L = SSD_CHUNK
    tail = SSD_CONV - 1
    base = 8 - tail

    @pl.when(c == 0)
    def _():
        ht_ref[...] = jnp.zeros_like(ht_ref)
        xp_ref[0:8, :] = jnp.zeros((8, SSD_CONV_DIM), F32)

    xp_ref[8:8 + L, :] = xbc_ref[...]
    acc = jnp.broadcast_to(cb_ref[...], (L, SSD_CONV_DIM))
    for k in range(SSD_CONV):
        acc = acc + cw_ref[k:k + 1, :] * xp_ref[base + k:base + k + L, :]
    new_tail = xp_ref[8 + L - tail:8 + L, :]
    convout_ref[...] = new_tail
    xp_ref[base:8, :] = new_tail
    xa = _silu(acc)
    xs = xa[:, :D_SSD]

    dt = _softplus(dt_ref[...] + dtb_ref[...])
    a = -jnp.exp(alog_ref[...])
    da = dt * a
    row = lax.broadcasted_iota(jnp.int32, (L, L), 0)
    col = lax.broadcasted_iota(jnp.int32, (L, L), 1)
    causal = col <= row
    tril = causal.astype(BF16)
    a_cs = _dot_exact_lhs(tril, da)
    a_cs_t = a_cs.T
    dt_t = dt.T
    a_last = a_cs[L - 1:L, :]
    wgt = dt * jnp.exp(a_last - a_cs)
    expa = jnp.exp(a_cs)
    cdec = jnp.exp(a_last)

    hpg = N_SSD_HEADS // SSD_GROUPS
    for g in range(SSD_GROUPS):
        bg = xa[:, D_SSD + g * SSD_STATE:D_SSD + (g + 1) * SSD_STATE]
        cg = xa[:, D_SSD + (SSD_GROUPS + g) * SSD_STATE:D_SSD + (SSD_GROUPS + g + 1) * SSD_STATE]
        bgb = bg.astype(BF16)
        cgb = cg.astype(BF16)
        bgt = bg.T.astype(BF16)
        cbm = _dot_nt(cgb, bgb)
        for e8 in range(hpg):
            e = g * hpg + e8
            sl = slice(e * SSD_HEAD_DIM, (e + 1) * SSD_HEAD_DIM)
            seg = a_cs[:, e:e + 1] - a_cs_t[e:e + 1, :]
            dec = jnp.exp(jnp.where(causal, seg, -jnp.inf))
            m = (cbm * dec * dt_t[e:e + 1, :]).astype(BF16)
            xe = xs[:, sl]
            y_diag = _dot(m, xe.astype(BF16))
            hte = ht_ref[:, sl]
            y_off = _dot(cgb, hte.astype(BF16)) * expa[:, e:e + 1]
            xw = (xe * wgt[:, e:e + 1]).astype(BF16)
            ht_ref[:, sl] = cdec[:, e:e + 1] * hte + _dot(bgt, xw)
            yacc_ref[:, sl] = y_diag + y_off

    y = yacc_ref[...] + dexp_ref[...] * xs
    y = y * _silu(z_ref[...])
    gw = D_SSD // SSD_GROUPS
    for g in range(SSD_GROUPS):
        yg = y[:, g * gw:(g + 1) * gw]
        yg = yg * lax.rsqrt(jnp.mean(yg * yg, axis=-1, keepdims=True) + EPS)
        y_ref[:, g * gw:(g + 1) * gw] = yg * ng_ref[:, g * gw:(g + 1) * gw]

    @pl.when(c == pl.num_programs(1) - 1)
    def _():
        hout_ref[...] = ht_ref[...].T


def _ssd_call(xbc, dt, z, p, *, nb, t):
    nc = t // SSD_CHUNK
    L = SSD_CHUNK
    row = lambda b, c: (b * nc + c, 0)
    const = lambda b, c: (0, 0)
    return pl.pallas_call(
        _ssd_kernel,
        grid=(nb, nc),
        in_specs=[pl.BlockSpec((L, SSD_CONV_DIM), row), pl.BlockSpec((L, DT_PAD), row),
                  pl.BlockSpec((L, D_SSD), row),
                  pl.BlockSpec((SSD_CONV, SSD_CONV_DIM), const), pl.BlockSpec((1, SSD_CONV_DIM), const),
                  pl.BlockSpec((1, DT_PAD), const), pl.BlockSpec((1, DT_PAD), const),
                  pl.BlockSpec((1, D_SSD), const), pl.BlockSpec((1, D_SSD), const)],
        out_specs=[pl.BlockSpec((L, D_SSD), row),
                   pl.BlockSpec((None, D_SSD, SSD_STATE), lambda b, c: (b, 0, 0)),
                   pl.BlockSpec((None, SSD_CONV - 1, SSD_CONV_DIM), lambda b, c: (b, 0, 0))],
        out_shape=[jax.ShapeDtypeStruct((nb * t, D_SSD), F32),
                   jax.ShapeDtypeStruct((nb, D_SSD, SSD_STATE), F32),
                   jax.ShapeDtypeStruct((nb, SSD_CONV - 1, SSD_CONV_DIM), F32)],
        scratch_shapes=[pltpu.VMEM((8 + L, SSD_CONV_DIM), F32),
                        pltpu.VMEM((SSD_STATE, D_SSD), F32),
                        pltpu.VMEM((L, D_SSD), F32)],
        compiler_params=_cparams("arbitrary", "arbitrary"),
        name="ssd_prompt",
    )(xbc, dt, z, p["ssd_conv_w"], p["ssd_conv_b"], p["dtb_pad"], p["alog_pad"], p["d_exp"], p["ssd_norm_g"])


def _ssd_step_kernel(xbc_ref, dt_ref, z_ref, conv_ref, h_ref, cw_ref, cb_ref, dtb_ref, alog_ref, dexp_ref,
                     ng_ref, e_ref, y_ref, hout_ref, convout_ref, r_ref):
    x = xbc_ref[...]
    buf = conv_ref[...]
    tail = SSD_CONV - 1
    acc = cb_ref[...] + cw_ref[tail:tail + 1, :] * x
    for k in range(tail):
        acc = acc + cw_ref[k:k + 1, :] * buf[k:k + 1, :]
    convout_ref[0:tail - 1, :] = buf[1:tail, :]
    convout_ref[tail - 1:tail, :] = x
    xa = _silu(acc)
    xs = xa[:, :D_SSD]

    dt = _softplus(dt_ref[...] + dtb_ref[...])
    dt_e = _dot_exact_rhs(jnp.broadcast_to(dt, (8, DT_PAD)), e_ref[...])[0:1, :]
    d_a = jnp.exp(dt_e * (-jnp.exp(alog_ref[...])))
    xdt = xs * dt_e

    @pl.when(pl.program_id(0) == 0)
    def _():
        r_ref[...] = jnp.zeros_like(r_ref)

    r_ref[0:1, :] = xdt
    r_ref[1:2, :] = d_a
    rt = r_ref[...].T
    gw = D_SSD // SSD_GROUPS
    for g in range(SSD_GROUPS):
        rows = slice(g * gw, (g + 1) * gw)
        bg = xa[:, D_SSD + g * SSD_STATE:D_SSD + (g + 1) * SSD_STATE]
        cg = xa[:, D_SSD + (SSD_GROUPS + g) * SSD_STATE:D_SSD + (SSD_GROUPS + g + 1) * SSD_STATE]
        hn = rt[rows, 1:2] * h_ref[rows, :] + rt[rows, 0:1] * bg
        hout_ref[rows, :] = hn
        c8 = jnp.broadcast_to(cg, (8, SSD_STATE)).astype(BF16)
        yg = _dot_nt(c8, hn.astype(BF16))[0:1, :]
        yg = yg + dexp_ref[:, rows] * xs[:, rows]
        yg = yg * _silu(z_ref[:, rows])
        yg = yg * lax.rsqrt(jnp.mean(yg * yg, axis=-1, keepdims=True) + EPS)
        y_ref[:, rows] = yg * ng_ref[:, rows]


def _ssd_step_call(xbc, dt, z, conv_state, ssm_state, p, expand):
    bs = xbc.shape[0]
    per_b = lambda b: (b, 0, 0)
    const = lambda b: (0, 0)
    tail = SSD_CONV - 1
    y, h, cv = pl.pallas_call(
        _ssd_step_kernel,
        grid=(bs,),
        in_specs=[pl.BlockSpec((None, 1, SSD_CONV_DIM), per_b), pl.BlockSpec((None, 1, DT_PAD), per_b),
                  pl.BlockSpec((None, 1, D_SSD), per_b), pl.BlockSpec((None, tail, SSD_CONV_DIM), per_b),
                  pl.BlockSpec((None, D_SSD, SSD_STATE), per_b),
                  pl.BlockSpec((SSD_CONV, SSD_CONV_DIM), const), pl.BlockSpec((1, SSD_CONV_DIM), const),
                  pl.BlockSpec((1, DT_PAD), const), pl.BlockSpec((1, D_SSD), const),
                  pl.BlockSpec((1, D_SSD), const), pl.BlockSpec((1, D_SSD), const),
                  pl.BlockSpec((DT_PAD, D_SSD), const)],
        out_specs=[pl.BlockSpec((None, 1, D_SSD), per_b), pl.BlockSpec((None, D_SSD, SSD_STATE), per_b),
                   pl.BlockSpec((None, tail, SSD_CONV_DIM), per_b)],
        out_shape=[jax.ShapeDtypeStruct((bs, 1, D_SSD), F32),
                   jax.ShapeDtypeStruct((bs, D_SSD, SSD_STATE), F32),
                   jax.ShapeDtypeStruct((bs, tail, SSD_CONV_DIM), F32)],
        scratch_shapes=[pltpu.VMEM((LANES, D_SSD), F32)],
        compiler_params=_cparams("arbitrary"),
        name="ssd_step",
    )(xbc.reshape(bs, 1, SSD_CONV_DIM), dt.reshape(bs, 1, DT_PAD), z.reshape(bs, 1, D_SSD), conv_state,
      ssm_state.reshape(bs, D_SSD, SSD_STATE), p["ssd_conv_w"], p["ssd_conv_b"], p["dtb_pad"],
      p["alog_exp"], p["d_exp"], p["ssd_norm_g"], expand)
    return y.reshape(bs, D_SSD), h, cv


def _conf_tail(u, gc, lg_ref, lb_ref, pw_ref):
    mu = jnp.mean(u, axis=-1, keepdims=True)
    uc = u - mu
    var = jnp.mean(uc * uc, axis=-1, keepdims=True)
    un = _silu(uc * lax.rsqrt(var + EPS) * lg_ref[...] + lb_ref[...])
    return _dot(un.astype(BF16), pw_ref[...]) * _silu(gc)


def _conf_kernel(glu_ref, w_ref, b_ref, lg_ref, lb_ref, pw_ref, y_ref, bufout_ref, ub_ref, cv_ref, sh_ref,
                 *, tb, rb):
    c = pl.program_id(1)
    hist = CONF_WIDTH - 1
    pad = CONF_PAD
    sub = 8

    @pl.when(c == 0)
    def _():
        ub_ref[0:pad, :] = jnp.zeros((pad, D_CONF), F32)

    ub_ref[pad:pad + tb, :] = glu_ref[:, 0:D_CONF] * jax.nn.sigmoid(glu_ref[:, D_CONF:2 * D_CONF])
    n_sh = pad + tb - sub
    for s in range(1, sub):
        sh_ref[s, 0:n_sh, :] = ub_ref[s:s + n_sh, :]
    for r in range(tb // rb):
        acc = jnp.broadcast_to(b_ref[...], (rb, D_CONF))
        for k in range(CONF_WIDTH):
            off = pad - hist + k
            s = off % sub
            lo = off - s + r * rb
            src = ub_ref[lo:lo + rb, :] if s == 0 else sh_ref[s, lo:lo + rb, :]
            acc = acc + w_ref[k:k + 1, :] * src
        cv_ref[r * rb:(r + 1) * rb, :] = acc
    new_hist = ub_ref[pad + tb - hist:pad + tb, :]
    bufout_ref[...] = new_hist
    ub_ref[pad - hist:pad, :] = new_hist
    y_ref[...] = _conf_tail(cv_ref[...], glu_ref[:, 2 * D_CONF:3 * D_CONF], lg_ref, lb_ref, pw_ref)


def _conf_call(glu, p, *, nb, t):
    tb = min(256, t)
    nt = t // tb
    hist = CONF_WIDTH - 1
    row = lambda b, c: (b * nt + c, 0)
    const = lambda b, c: (0, 0)
    return pl.pallas_call(
        functools.partial(_conf_kernel, tb=tb, rb=32),
        grid=(nb, nt),
        in_specs=[pl.BlockSpec((tb, 3 * D_CONF), row),
                  pl.BlockSpec((CONF_WIDTH, D_CONF), const), pl.BlockSpec((1, D_CONF), const),
                  pl.BlockSpec((1, D_CONF), const), pl.BlockSpec((1, D_CONF), const),
                  pl.BlockSpec((D_CONF, D_CONF), const)],
        out_specs=[pl.BlockSpec((tb, D_CONF), row),
                   pl.BlockSpec((None, hist, D_CONF), lambda b, c: (b, 0, 0))],
        out_shape=[jax.ShapeDtypeStruct((nb * t, D_CONF), F32),
                   jax.ShapeDtypeStruct((nb, hist, D_CONF), F32)],
        scratch_shapes=[pltpu.VMEM((CONF_PAD + tb, D_CONF), F32), pltpu.VMEM((tb, D_CONF), F32),
                        pltpu.VMEM((8, CONF_PAD + tb, D_CONF), F32)],
        compiler_params=_cparams("arbitrary", "arbitrary"),
        name="conformer_prompt",
    )(glu, p["conf_conv_w"], p["conf_conv_b"], p["conf_ln_g"], p["conf_ln_b"], p["pw_bf16"])


def _conf_step_kernel(glu_ref, buf_ref, w_ref, b_ref, lg_ref, lb_ref, pw_ref, y_ref, bufout_ref):
    hist = CONF_WIDTH - 1
    u = glu_ref[:, 0:D_CONF] * jax.nn.sigmoid(glu_ref[:, D_CONF:2 * D_CONF])
    acc = b_ref[...] + w_ref[hist:hist + 1, :] * u
    for k in range(hist):
        acc = acc + w_ref[k:k + 1, :] * buf_ref[k]
    for k in range(hist - 1):
        bufout_ref[k] = buf_ref[k + 1]
    bufout_ref[hist - 1] = u
    y_ref[...] = _conf_tail(acc, glu_ref[:, 2 * D_CONF:3 * D_CONF], lg_ref, lb_ref, pw_ref)


def _conf_step_call(glu, buf_tap_major, p):
    bs = glu.shape[0]
    hist = CONF_WIDTH - 1
    return pl.pallas_call(
        _conf_step_kernel,
        out_shape=[jax.ShapeDtypeStruct((bs, D_CONF), F32),
                   jax.ShapeDtypeStruct((hist, bs, D_CONF), F32)],
        compiler_params=pltpu.CompilerParams(vmem_limit_bytes=VMEM_LIMIT),
        name="conformer_step",
    )(glu, buf_tap_major, p["conf_conv_w"], p["conf_conv_b"], p["conf_ln_g"], p["conf_ln_b"], p["pw_bf16"])


def _att_kernel(bias_ref, q_ref, kb_ref, vb_ref, ga_ref, o_ref, acc_ref, nrsp_ref, z0_ref, z1_ref, w0_ref, w1_ref,
                *, tq, hps):
    grp = pl.program_id(1)
    qi = pl.program_id(2)
    hd = ATT_HEAD_DIM

    q = (q_ref[...] * (hd ** -0.5 * LOG2E)).astype(BF16)
    row = lax.broadcasted_iota(jnp.int32, (tq, tq), 0)
    col = lax.broadcasted_iota(jnp.int32, (tq, tq), 1)
    causal = col < row
    later = (row > col).astype(BF16)
    lane = lax.broadcasted_iota(jnp.int32, (tq, hd), 1)
    ones_rows = jnp.ones((hd, tq), BF16)
    qhs = []
    for hh in range(hps):
        parts = _split3(jnp.full((tq, hd), bias_ref[grp * hps + hh] * LOG2E, F32))
        extra = jnp.zeros((tq, hd), F32)
        for idx, part in enumerate(parts):
            extra = jnp.where(lane == idx, part.astype(F32), extra)
        qhs.append(jnp.concatenate([q[:, hh * hd:(hh + 1) * hd], extra.astype(BF16)], axis=1))

    heads = range(hps)
    rows = [slice(hh * hd, (hh + 1) * hd) for hh in heads]

    z_refs = (z0_ref, z1_ref)
    w_refs = (w0_ref, w1_ref)

    def logits(hh, j, slot):
        z_refs[slot][hh] = _dot(qhs[hh], jnp.concatenate([kb_ref[j, rows[hh], :], ones_rows], axis=0))

    def accumulate(hh, j, slot):
        acc_ref[hh] = acc_ref[hh] + _dot_nt(w_refs[slot][hh], vb_ref[j, rows[hh], :])

    def weights(hh, masked, slot):
        for r in range(tq // ATT_ROW_BLOCK):
            rs = slice(r * ATT_ROW_BLOCK, (r + 1) * ATT_ROW_BLOCK)
            z = z_refs[slot][hh, rs, :]
            neg_abs = lax.bitcast_convert_type(lax.bitcast_convert_type(z, jnp.uint32) | jnp.uint32(0x80000000), F32)
            sp = jnp.maximum(z, 0.0) + jnp.log(1.0 + jnp.exp2(neg_abs)) * LOG2E
            nrsp = nrsp_ref[hh, rs, :]
            lsig = z - sp + nrsp
            if masked:
                sp = jnp.where(causal[rs, :], sp, 0.0)
            csp = _dot(sp.astype(BF16), later)
            w = jnp.exp2(lsig - csp)
            if masked:
                w = jnp.where(causal[rs, :], w, 0.0)
            w_refs[slot][hh, rs, :] = w.astype(BF16)
            nrsp_ref[hh, rs, :] = nrsp - (csp[:, 0:1] + sp[:, 0:1])

    acc_ref[...] = jnp.zeros_like(acc_ref)
    nrsp_ref[...] = jnp.zeros_like(nrsp_ref)
    for hh in heads:
        logits(hh, qi, 0)
    for hh in heads:
        weights(hh, True, 0)
        logits(hh, jnp.maximum(qi - 1, 0), 1)

    def step(j, slot):
        for hh in heads:
            accumulate(hh, j + 1, 1 - slot)
            logits(hh, jnp.maximum(j - 1, 0), 1 - slot)
            weights(hh, False, slot)

    @pl.loop(0, qi // 2)
    def _(it):
        j = qi - 1 - 2 * it
        step(j, 1)
        step(j - 1, 0)

    @pl.when(qi % 2 == 1)
    def _():
        step(0, 1)

    for parity in range(2):
        @pl.when(qi % 2 == parity)
        def _():
            for hh in heads:
                accumulate(hh, 0, parity)
    o = jnp.concatenate([acc_ref[hh] for hh in heads], axis=1)
    o_ref[...] = o * _silu(ga_ref[...])


def _att_call(q, kb, vb, ga, bias, *, nb, t, tq):
    nq = t // tq
    hps = ATT_HEADS_PER_STEP
    width = hps * ATT_HEAD_DIM
    qspec = pl.BlockSpec((tq, width), lambda b, g, i: (b * nq + i, g))
    kspec = pl.BlockSpec((None, nq, width, tq), lambda b, g, i: (b, 0, g, 0))
    return pl.pallas_call(
        functools.partial(_att_kernel, tq=tq, hps=hps),
        grid=(nb, N_ATT_HEADS // hps, nq),
        in_specs=[pl.BlockSpec(memory_space=pltpu.SMEM), qspec, kspec, kspec, qspec],
        out_specs=qspec,
        out_shape=jax.ShapeDtypeStruct((nb * t, D_ATT), F32),
        scratch_shapes=[pltpu.VMEM((hps, tq, ATT_HEAD_DIM), F32), pltpu.VMEM((hps, tq, 1), F32),
                        pltpu.VMEM((hps, tq, tq), F32), pltpu.VMEM((hps, tq, tq), F32),
                        pltpu.VMEM((hps, tq, tq), BF16), pltpu.VMEM((hps, tq, tq), BF16)],
        compiler_params=_cparams("arbitrary", "arbitrary", "arbitrary"),
        name="stickbreak_prompt",
    )(bias, q, kb, vb, ga)


def _att_dec_kernel(pt_ref, bias_ref, q_ref, ga_ref, *refs, npg):
    del pt_ref
    k_refs = refs[:npg]
    v_refs = refs[npg:2 * npg]
    o_ref, acc_ref, rsp_ref = refs[2 * npg:]
    j = pl.program_id(1)
    nh = N_ATT_HEADS

    @pl.when(j == 0)
    def _():
        acc_ref[...] = jnp.zeros_like(acc_ref)
        rsp_ref[...] = jnp.zeros_like(rsp_ref)

    head = lax.broadcasted_iota(jnp.int32, (nh, D_ATT), 0)
    lane = lax.broadcasted_iota(jnp.int32, (nh, D_ATT), 1)
    own = lane // ATT_HEAD_DIM == head
    q = q_ref[...] * (ATT_HEAD_DIM ** -0.5)
    qbd = jnp.where(own, jnp.broadcast_to(q, (nh, D_ATT)), 0.0).astype(BF16)
    bias = bias_ref[...]
    zs = [_dot(qbd, k_refs[i][...].astype(BF16)) + bias for i in range(npg)]
    sps = [_softplus(z) for z in zs]
    row = lax.broadcasted_iota(jnp.int32, (PAGE_SIZE, PAGE_SIZE), 0)
    col = lax.broadcasted_iota(jnp.int32, (PAGE_SIZE, PAGE_SIZE), 1)
    later = (row > col).astype(BF16)
    loc = _dot_exact_rhs(jnp.concatenate(sps, axis=0), later)
    run = rsp_ref[...]
    acc = acc_ref[...]
    for i in reversed(range(npg)):
        w = jnp.exp(zs[i] - sps[i] - loc[i * nh:(i + 1) * nh, :] - run)
        acc = acc + _dot_nt(w.astype(BF16), v_refs[i][...].astype(BF16))
        run = run + jnp.sum(sps[i], axis=-1, keepdims=True)
    acc_ref[...] = acc
    rsp_ref[...] = run

    @pl.when(j == pl.num_programs(1) - 1)
    def _():
        o = jnp.sum(jnp.where(own, acc, 0.0), axis=0, keepdims=True)
        o_ref[...] = o * _silu(ga_ref[...])


def _att_dec_call(q, ga, cache_k, cache_v, page_table, bias, layer):
    bs, n_pages = page_table.shape
    npg = min(PAGES_PER_STEP, n_pages)
    ng = n_pages // npg
    depth, n_pool = cache_k.shape[:2]
    ck = jnp.transpose(cache_k, (0, 1, 3, 4, 2)).reshape(depth, n_pool, D_ATT, PAGE_SIZE)
    cv = jnp.transpose(cache_v, (0, 1, 3, 4, 2)).reshape(depth, n_pool, D_ATT, PAGE_SIZE)

    def page_spec(i):
        return pl.BlockSpec((None, None, D_ATT, PAGE_SIZE),
                            lambda b, j, pt: (layer, pt[b, (ng - 1 - j) * npg + i], 0, 0))

    per_b = lambda b, j, pt: (b, 0, 0)
    grid_spec = pltpu.PrefetchScalarGridSpec(
        num_scalar_prefetch=1,
        grid=(bs, ng),
        in_specs=[pl.BlockSpec((N_ATT_HEADS, 1), lambda b, j, pt: (0, 0)),
                  pl.BlockSpec((None, 1, D_ATT), per_b), pl.BlockSpec((None, 1, D_ATT), per_b)]
                 + [page_spec(i) for i in range(npg)] * 2,
        out_specs=pl.BlockSpec((None, 1, D_ATT), per_b),
        scratch_shapes=[pltpu.VMEM((N_ATT_HEADS, D_ATT), F32), pltpu.VMEM((N_ATT_HEADS, 1), F32)],
    )
    out = pl.pallas_call(
        functools.partial(_att_dec_kernel, npg=npg),
        grid_spec=grid_spec,
        out_shape=jax.ShapeDtypeStruct((bs, 1, D_ATT), F32),
        compiler_params=_cparams("arbitrary", "arbitrary"),
        name="stickbreak_paged",
    )(page_table, bias.reshape(N_ATT_HEADS, 1), q.reshape(bs, 1, D_ATT), ga.reshape(bs, 1, D_ATT),
      *([ck] * npg), *([cv] * npg))
    return out.reshape(bs, D_ATT)


def _outproj_kernel(ys_ref, yc_ref, ya_ref, x_ref, gate_ref, w_ref, fg_ref, o_ref, *, final):
    acc = _dot(ys_ref[...].astype(BF16), w_ref[0:D_SSD, :])
    acc = acc + _dot(yc_ref[...].astype(BF16), w_ref[D_SSD:D_SSD + D_CONF, :])
    acc = acc + _dot(ya_ref[...].astype(BF16), w_ref[D_SSD + D_CONF:, :])
    xn = x_ref[...] + gate_ref[...] * acc
    if final:
        xn = xn * lax.rsqrt(jnp.mean(xn * xn, axis=-1, keepdims=True) + EPS) * fg_ref[...]
    o_ref[...] = xn


def _outproj_call(ys, yc, ya, x, mod, w_bf16, final_g, *, tm, rows_per_batch, final):
    n, d = x.shape
    if rows_per_batch > 1:
        per = rows_per_batch // tm
        gate_spec = pl.BlockSpec((None, 1, d), lambda i: (i // per, 0, 2))
    else:
        gate_spec = pl.BlockSpec((tm, d), lambda i: (i, 2))
    row = lambda i: (i, 0)
    return pl.pallas_call(
        functools.partial(_outproj_kernel, final=final),
        grid=(n // tm,),
        in_specs=[pl.BlockSpec((tm, D_SSD), row), pl.BlockSpec((tm, D_CONF), row), pl.BlockSpec((tm, D_ATT), row),
                  pl.BlockSpec((tm, d), row), gate_spec,
                  pl.BlockSpec((D_SSD + D_CONF + D_ATT, d), lambda i: (0, 0)),
                  pl.BlockSpec((1, d), lambda i: (0, 0))],
        out_specs=pl.BlockSpec((tm, d), row),
        out_shape=jax.ShapeDtypeStruct((n, d), F32),
        compiler_params=_cparams("arbitrary"),
        name="outproj_residual",
    )(ys, yc, ya, x, mod, w_bf16, final_g.reshape(1, d))


def _prep_layer(l, w_in, ssd_conv_w, ssd_conv_b, ssd_dt_bias, ssd_a_log, ssd_d, ssd_norm_g, conf_conv_w,
                conf_conv_b, conf_ln_g, conf_ln_b, conf_w_pw, w_out):
    wt = jnp.swapaxes(w_in, 1, 2)[l]
    dt_lo = D_SSD + SSD_CONV_DIM
    dt_hi = dt_lo + N_SSD_HEADS
    w_r = jnp.concatenate([wt[:dt_lo], wt[dt_hi:],
                           jnp.pad(wt[dt_lo:dt_hi], ((0, DT_PAD - N_SSD_HEADS), (0, 0)))], axis=0)
    pad16 = lambda v: jnp.pad(v, (0, DT_PAD - N_SSD_HEADS)).reshape(1, DT_PAD)
    rep = lambda v: jnp.repeat(v, SSD_HEAD_DIM).reshape(1, D_SSD)
    return dict(
        w_in=w_r.astype(BF16),
        ssd_conv_w=ssd_conv_w[l], ssd_conv_b=ssd_conv_b[l].reshape(1, -1),
        dtb_pad=pad16(ssd_dt_bias[l]), alog_pad=pad16(ssd_a_log[l]), alog_exp=rep(ssd_a_log[l]),
        d_exp=rep(ssd_d[l]), ssd_norm_g=ssd_norm_g[l].reshape(1, -1),
        conf_conv_w=conf_conv_w[l], conf_conv_b=conf_conv_b[l].reshape(1, -1),
        conf_ln_g=conf_ln_g[l].reshape(1, -1), conf_ln_b=conf_ln_b[l].reshape(1, -1),
        pw_bf16=conf_w_pw[l].astype(BF16), w_out=w_out[l].astype(BF16))


def kernel(x_prompt, x_sample, c_prompt, c_sample, cache_k, cache_v, page_table, state_ssm, state_ssd_conv, state_conf_conv, norm_g, w_ada, b_ada, w_in, ssd_conv_w, ssd_conv_b, ssd_dt_bias, ssd_a_log, ssd_d, ssd_norm_g, conf_conv_w, conf_conv_b, conf_ln_g, conf_ln_b, conf_w_pw, att_logit_bias, w_out, final_norm_g):
    nb, t, d = x_prompt.shape
    bs = x_sample.shape[0]
    depth = w_in.shape[0]

    n_mod = nb + bs
    n_mod_pad = -(-n_mod // 8) * 8
    c_all = jnp.concatenate([c_prompt, c_sample, jnp.zeros((n_mod_pad - n_mod, d), F32)], axis=0)
    mod = _ada_call(c_all, w_ada, b_ada)

    lane_head = jnp.arange(D_SSD) // SSD_HEAD_DIM
    expand = (jnp.arange(DT_PAD)[:, None] == lane_head[None, :]).astype(BF16)

    xp = x_prompt.reshape(nb * t, d)
    xs = x_sample.reshape(bs, d)
    tm_p = min(ATT_TILE, t)
    outs_p = []
    outs_s = []
    for l in range(depth):
        p = _prep_layer(l, w_in, ssd_conv_w, ssd_conv_b, ssd_dt_bias, ssd_a_log, ssd_d, ssd_norm_g,
                        conf_conv_w, conf_conv_b, conf_ln_g, conf_ln_b, conf_w_pw, w_out)
        final = l == depth - 1
        mod_p = mod[l, :nb].reshape(nb, 1, 3 * d)
        mod_s = mod[l, nb:nb + bs]

        z, xbc, glu, q, kt, vt, ga, dt, kb, vb = _inproj_call(xp, mod_p, norm_g[l], p["w_in"], tm=tm_p,
                                                              rows_per_batch=t)
        y_ssd, h_p, cv_p = _ssd_call(xbc, dt, z, p, nb=nb, t=t)
        y_conf, cf_p = _conf_call(glu, p, nb=nb, t=t)
        y_att = _att_call(q, kb, vb, ga, att_logit_bias[l], nb=nb, t=t, tq=tm_p)
        xp = _outproj_call(y_ssd, y_conf, y_att, xp, mod_p, p["w_out"], final_norm_g, tm=tm_p,
                           rows_per_batch=t, final=final)
        heads_last = lambda a: jnp.transpose(a.reshape(nb, N_ATT_HEADS, ATT_HEAD_DIM, t), (0, 3, 1, 2))
        outs_p.append((heads_last(kt), heads_last(vt),
                       h_p.reshape(nb, N_SSD_HEADS, SSD_HEAD_DIM, SSD_STATE), cv_p, cf_p))

        z, xbc, glu, q, k, v, ga, dt = _inproj_call(xs, mod_s, norm_g[l], p["w_in"], tm=bs, rows_per_batch=1)
        y_ssd, h_s, cv_s = _ssd_step_call(xbc, dt, z, state_ssd_conv[l], state_ssm[l], p, expand)
        y_conf, cf_s = _conf_step_call(glu, jnp.swapaxes(state_conf_conv[l], 0, 1), p)
        y_att = _att_dec_call(q, ga, cache_k, cache_v, page_table, att_logit_bias[l], l)
        xs = _outproj_call(y_ssd, y_conf, y_att, xs, mod_s, p["w_out"], final_norm_g, tm=bs,
                           rows_per_batch=1, final=final)
        outs_s.append((k.reshape(bs, 1, N_ATT_HEADS, ATT_HEAD_DIM), v.reshape(bs, 1, N_ATT_HEADS, ATT_HEAD_DIM),
                       h_s.reshape(bs, N_SSD_HEADS, SSD_HEAD_DIM, SSD_STATE), cv_s, jnp.swapaxes(cf_s, 0, 1)))

    stack = lambda outs, i: jnp.stack([o[i] for o in outs])
    return (xp.reshape(nb, t, d), xs.reshape(bs, 1, d),
            stack(outs_p, 0), stack(outs_p, 1), stack(outs_s, 0), stack(outs_s, 1),
            stack(outs_p, 2), stack(outs_s, 2), stack(outs_p, 3), stack(outs_s, 3),
            stack(outs_p, 4), stack(outs_s, 4))
```

```python
import functools

import jax
import jax.numpy as jnp
from jax import lax
from jax.experimental import pallas as pl
from jax.experimental.pallas import tpu as pltpu

F32 = jnp.float32
BF16 = jnp.bfloat16
EPS = 1e-6
LOG2E = 1.4426950408889634

D_MODEL = 1024
D_SSD = 1024
SSD_HEAD_DIM = 64
N_SSD_HEADS = 16
SSD_GROUPS = 2
SSD_STATE = 128
SSD_CONV = 4
SSD_CONV_DIM = D_SSD + 2 * SSD_GROUPS * SSD_STATE
D_CONF = 512
CONF_WIDTH = 31
D_ATT = 512
N_ATT_HEADS = 8
ATT_HEAD_DIM = 64
PAGE_SIZE = 128

LANES = 128
SSD_CHUNK = 128
ATT_TILE = 256
ATT_HEADS_PER_STEP = 4
CONF_PAD = 32
PAGES_PER_STEP = 32
ATT_ROW_BLOCK = 128
DT_PAD = LANES
PROJ_COLS = (("z", D_SSD), ("xbc", SSD_CONV_DIM), ("glu", 3 * D_CONF), ("q", D_ATT), ("k", D_ATT),
             ("v", D_ATT), ("ga", D_ATT), ("dt", DT_PAD))
_DT_ROW = D_SSD + SSD_CONV_DIM
_GLU_ROW = _DT_ROW + N_SSD_HEADS
PROJ_ROWS = dict(z=0, xbc=D_SSD, dt=_DT_ROW, glu=_GLU_ROW, q=_GLU_ROW + 3 * D_CONF,
                 k=_GLU_ROW + 3 * D_CONF + D_ATT, v=_GLU_ROW + 3 * D_CONF + 2 * D_ATT,
                 ga=_GLU_ROW + 3 * D_CONF + 3 * D_ATT)
D_PROJ = _GLU_ROW + 3 * D_CONF + 4 * D_ATT
VMEM_LIMIT = 56 * 1024 * 1024


def _cparams(*sem):
    return pltpu.CompilerParams(dimension_semantics=sem, vmem_limit_bytes=VMEM_LIMIT)


def _silu(x):
    return x * jax.nn.sigmoid(x)


def _softplus(x):
    return jnp.maximum(x, 0.0) + jnp.log(1.0 + jnp.exp(-jnp.abs(x)))


def _split3(v):
    hi = v.astype(BF16)
    r = v - hi.astype(F32)
    mid = r.astype(BF16)
    lo = (r - mid.astype(F32)).astype(BF16)
    return hi, mid, lo


def _dot(a, b):
    return jnp.dot(a, b, preferred_element_type=F32)


def _dot_nt(a, b):
    return lax.dot_general(a, b, (((1,), (1,)), ((), ())), preferred_element_type=F32)


def _dot_exact_rhs(v, m):
    hi, mid, lo = _split3(v)
    return _dot(hi, m) + _dot(mid, m) + _dot(lo, m)


def _dot_exact_lhs(m, v):
    hi, mid, lo = _split3(v)
    return _dot(m, hi) + _dot(m, mid) + _dot(m, lo)


def _ada_kernel(c_ref, w_ref, b_ref, o_ref):
    s = _silu(c_ref[...]).astype(BF16)
    o_ref[...] = _dot(s, w_ref[...].astype(BF16)) + b_ref[...]


def _ada_call(c_all, w_ada, b_ada):
    depth, d, d3 = w_ada.shape
    rows = c_all.shape[0]
    tn = 1024
    return pl.pallas_call(
        _ada_kernel,
        grid=(depth, d3 // tn),
        in_specs=[pl.BlockSpec((rows, d), lambda l, j: (0, 0)),
                  pl.BlockSpec((None, d, tn), lambda l, j: (l, 0, j)),
                  pl.BlockSpec((None, 1, tn), lambda l, j: (l, 0, j))],
        out_specs=pl.BlockSpec((None, rows, tn), lambda l, j: (l, 0, j)),
        out_shape=jax.ShapeDtypeStruct((depth, rows, d3), F32),
        compiler_params=_cparams("arbitrary", "arbitrary"),
        name="ada_mod",
    )(c_all, w_ada, b_ada.reshape(depth, 1, d3))


def _inproj_kernel(x_ref, shift_ref, scale_ref, g_ref, wt_ref, *refs, kv_transposed, n_carried):
    out_refs = refs[n_carried:]
    x = x_ref[...]
    xn = x * lax.rsqrt(jnp.mean(x * x, axis=-1, keepdims=True) + EPS) * g_ref[...]
    h = (xn * (1.0 + scale_ref[...]) + shift_ref[...]).astype(BF16)
    bf16_refs = dict(zip(("k", "v"), out_refs[len(PROJ_COLS):]))
    for (name, width), o_ref in zip(PROJ_COLS, out_refs):
        w_rows = wt_ref[PROJ_ROWS[name]:PROJ_ROWS[name] + width, :]
        if kv_transposed and name in ("k", "v"):
            r = _dot_nt(w_rows, h)
            o_ref[...] = r
            bf16_refs[name][...] = r.astype(BF16)
        else:
            o_ref[...] = _dot_nt(h, w_rows)


def _inproj_call(x, mod, norm_g, wt_bf16, *, tm, rows_per_batch, layer=0, depth=1, kv_stack=None):
    n, d = x.shape
    kv_transposed = rows_per_batch > 1
    out_specs = [pl.BlockSpec((tm, w), lambda i: (i, 0)) for _, w in PROJ_COLS]
    out_shape = [jax.ShapeDtypeStruct((n, w), F32) for _, w in PROJ_COLS]
    carried = ()
    aliases = {}
    if kv_transposed:
        per = rows_per_batch // tm
        nb = n // rows_per_batch
        shift_spec = pl.BlockSpec((None, 1, d), lambda i: (i // per, 0, 0))
        scale_spec = pl.BlockSpec((None, 1, d), lambda i: (i // per, 0, 1))
        for idx, (name, w) in enumerate(PROJ_COLS):
            if name in ("k", "v"):
                out_specs[idx] = pl.BlockSpec((None, None, w, tm), lambda i: (layer, i // per, 0, i % per))
                out_shape[idx] = jax.ShapeDtypeStruct((depth, nb, w, rows_per_batch), F32)
        for _ in ("k", "v"):
            out_specs.append(pl.BlockSpec((None, None, D_ATT, tm), lambda i: (i // per, i % per, 0, 0)))
            out_shape.append(jax.ShapeDtypeStruct((nb, per, D_ATT, tm), BF16))
        if kv_stack is not None:
            carried = tuple(kv_stack)
            names = [name for name, _ in PROJ_COLS]
            aliases = {5: names.index("k"), 6: names.index("v")}
    else:
        shift_spec = pl.BlockSpec((tm, d), lambda i: (i, 0))
        scale_spec = pl.BlockSpec((tm, d), lambda i: (i, 1))
    return pl.pallas_call(
        functools.partial(_inproj_kernel, kv_transposed=kv_transposed, n_carried=len(carried)),
        grid=(n // tm,),
        in_specs=[pl.BlockSpec((tm, d), lambda i: (i, 0)), shift_spec, scale_spec,
                  pl.BlockSpec((1, d), lambda i: (0, 0)),
                  pl.BlockSpec((D_PROJ, d), lambda i: (0, 0), pipeline_mode=pl.Buffered(1))]
                 + [pl.BlockSpec(memory_space=pl.ANY)] * len(carried),
        out_specs=out_specs,
        out_shape=out_shape,
        input_output_aliases=aliases,
        compiler_params=_cparams("arbitrary"),
        name="norm_inproj",
    )(x, mod, mod, norm_g.reshape(1, d), wt_bf16, *carried)


def _ssd_kernel(xbc_ref, dt_ref, z_ref, cw_ref, cb_ref, dtb_ref, alog_ref, dexp_ref, ng_ref, e_ref,
                y_ref, hout_ref, convout_ref, xp_ref, ht_ref, yacc_ref):
    c = pl.program_id(1)
    L = SSD_CHUNK
    tail = SSD_CONV - 1
    base = 8 - tail

    @pl.when(c == 0)
    def _():
        ht_ref[...] = jnp.zeros_like(ht_ref)
        xp_ref[0:8, :] = jnp.zeros((8, SSD_CONV_DIM), F32)

    xp_ref[8:8 + L, :] = xbc_ref[...]
    acc = jnp.broadcast_to(cb_ref[...], (L, SSD_CONV_DIM))
    for k in range(SSD_CONV):
        acc = acc + cw_ref[k:k + 1, :] * xp_ref[base + k:base + k + L, :]
    new_tail = xp_ref[8 + L - tail:8 + L, :]
    convout_ref[...] = new_tail
    xp_ref[base:8, :] = new_tail
    xa = _silu(acc)
    xs = xa[:, :D_SSD]

    dt = _softplus(dt_ref[...] + dtb_ref[...])
    a = -jnp.exp(alog_ref[...])
    da = dt * a
    row = lax.broadcasted_iota(jnp.int32, (L, L), 0)
    col = lax.broadcasted_iota(jnp.int32, (L, L), 1)
    causal = col <= row
    tril = causal.astype(BF16)
    a_cs = _dot_exact_lhs(tril, da)
    a_cs_t = a_cs.T
    dt_t = dt.T
    a_last = a_cs[L - 1:L, :]
    expa_x = _dot_exact_rhs(jnp.exp(a_cs), e_ref[...])
    wgt_x = _dot_exact_rhs(dt * jnp.exp(a_last - a_cs), e_ref[...])
    cdec_x = expa_x[L - 1:L, :]

    hpg = N_SSD_HEADS // SSD_GROUPS
    gw = D_SSD // SSD_GROUPS
    pair_lane = lax.broadcasted_iota(jnp.int32, (L, 2 * SSD_HEAD_DIM), 1)
    for g in range(SSD_GROUPS):
        gs = slice(g * gw, (g + 1) * gw)
        bg = xa[:, D_SSD + g * SSD_STATE:D_SSD + (g + 1) * SSD_STATE]
        cg = xa[:, D_SSD + (SSD_GROUPS + g) * SSD_STATE:D_SSD + (SSD_GROUPS + g + 1) * SSD_STATE]
        bgb = bg.astype(BF16)
        cgb = cg.astype(BF16)
        bgt = bg.T.astype(BF16)
        cbm = _dot_nt(cgb, bgb)
        htg = ht_ref[:, gs]
        y_off = _dot(cgb, htg.astype(BF16)) * expa_x[:, gs]
        xw = (xs[:, gs] * wgt_x[:, gs]).astype(BF16)
        ht_ref[:, gs] = cdec_x[:, gs] * htg + _dot(bgt, xw)
        for pr in range(hpg // 2):
            ms = []
            for e in (g * hpg + 2 * pr, g * hpg + 2 * pr + 1):
                seg = a_cs[:, e:e + 1] - a_cs_t[e:e + 1, :]
                dec = jnp.exp(jnp.where(causal, seg, -jnp.inf))
                ms.append((cbm * dec * dt_t[e:e + 1, :]).astype(BF16))
            ps = slice(g * gw + pr * 2 * SSD_HEAD_DIM, g * gw + (pr + 1) * 2 * SSD_HEAD_DIM)
            x_pair = xs[:, ps]
            x_bd = jnp.concatenate([jnp.where(pair_lane < SSD_HEAD_DIM, x_pair, 0.0),
                                    jnp.where(pair_lane >= SSD_HEAD_DIM, x_pair, 0.0)], axis=0).astype(BF16)
            y_diag = _dot(jnp.concatenate(ms, axis=1), x_bd)
            yacc_ref[:, ps] = y_diag + y_off[:, pr * 2 * SSD_HEAD_DIM:(pr + 1) * 2 * SSD_HEAD_DIM]

    y = yacc_ref[...] + dexp_ref[...] * xs
    y = y * _silu(z_ref[...])
    gw = D_SSD // SSD_GROUPS
    for g in range(SSD_GROUPS):
        yg = y[:, g * gw:(g + 1) * gw]
        yg = yg * lax.rsqrt(jnp.mean(yg * yg, axis=-1, keepdims=True) + EPS)
        y_ref[:, g * gw:(g + 1) * gw] = yg * ng_ref[:, g * gw:(g + 1) * gw]

    @pl.when(c == pl.num_programs(1) - 1)
    def _():
        hout_ref[...] = ht_ref[...].T


def _ssd_call(xbc, dt, z, p, expand, *, nb, t):
    nc = t // SSD_CHUNK
    L = SSD_CHUNK
    row = lambda b, c: (b * nc + c, 0)
    const = lambda b, c: (0, 0)
    return pl.pallas_call(
        _ssd_kernel,
        grid=(nb, nc),
        in_specs=[pl.BlockSpec((L, SSD_CONV_DIM), row), pl.BlockSpec((L, DT_PAD), row),
                  pl.BlockSpec((L, D_SSD), row),
                  pl.BlockSpec((SSD_CONV, SSD_CONV_DIM), const), pl.BlockSpec((1, SSD_CONV_DIM), const),
                  pl.BlockSpec((1, DT_PAD), const), pl.BlockSpec((1, DT_PAD), const),
                  pl.BlockSpec((1, D_SSD), const), pl.BlockSpec((1, D_SSD), const),
                  pl.BlockSpec((DT_PAD, D_SSD), const)],
        out_specs=[pl.BlockSpec((L, D_SSD), row),
                   pl.BlockSpec((None, D_SSD, SSD_STATE), lambda b, c: (b, 0, 0)),
                   pl.BlockSpec((None, SSD_CONV - 1, SSD_CONV_DIM), lambda b, c: (b, 0, 0))],
        out_shape=[jax.ShapeDtypeStruct((nb * t, D_SSD), F32),
                   jax.ShapeDtypeStruct((nb, D_SSD, SSD_STATE), F32),
                   jax.ShapeDtypeStruct((nb, SSD_CONV - 1, SSD_CONV_DIM), F32)],
        scratch_shapes=[pltpu.VMEM((8 + L, SSD_CONV_DIM), F32),
                        pltpu.VMEM((SSD_STATE, D_SSD), F32),
                        pltpu.VMEM((L, D_SSD), F32)],
        compiler_params=_cparams("arbitrary", "arbitrary"),
        name="ssd_prompt",
    )(xbc, dt, z, p["ssd_conv_w"], p["ssd_conv_b"], p["dtb_pad"], p["alog_pad"], p["d_exp"], p["ssd_norm_g"],
      expand)


def _ssd_step_kernel(xbc_ref, dt_ref, z_ref, conv_ref, h_ref, cw_ref, cb_ref, dtb_ref, alog_ref, dexp_ref,
                     ng_ref, e_ref, *refs, n_carried):
    y_ref, hout_ref, convout_ref, r_ref = refs[n_carried:]
    x = xbc_ref[...]
    buf = conv_ref[...]
    tail = SSD_CONV - 1
    acc = cb_ref[...] + cw_ref[tail:tail + 1, :] * x
    for k in range(tail):
        acc = acc + cw_ref[k:k + 1, :] * buf[k:k + 1, :]
    convout_ref[0:tail - 1, :] = buf[1:tail, :]
    convout_ref[tail - 1:tail, :] = x
    xa = _silu(acc)
    xs = xa[:, :D_SSD]

    dt = _softplus(dt_ref[...] + dtb_ref[...])
    dt_e = _dot_exact_rhs(jnp.broadcast_to(dt, (8, DT_PAD)), e_ref[...])[0:1, :]
    d_a = jnp.exp(dt_e * (-jnp.exp(alog_ref[...])))
    xdt = xs * dt_e

    @pl.when(pl.program_id(0) == 0)
    def _():
        r_ref[...] = jnp.zeros_like(r_ref)

    r_ref[0:1, :] = xdt
    r_ref[1:2, :] = d_a
    rt = r_ref[...].T
    gw = D_SSD // SSD_GROUPS
    for g in range(SSD_GROUPS):
        rows = slice(g * gw, (g + 1) * gw)
        bg = xa[:, D_SSD + g * SSD_STATE:D_SSD + (g + 1) * SSD_STATE]
        cg = xa[:, D_SSD + (SSD_GROUPS + g) * SSD_STATE:D_SSD + (SSD_GROUPS + g + 1) * SSD_STATE]
        hn = rt[rows, 1:2] * h_ref[rows, :] + rt[rows, 0:1] * bg
        hout_ref[rows, :] = hn
        c8 = jnp.broadcast_to(cg, (8, SSD_STATE)).astype(BF16)
        yg = _dot_nt(c8, hn.astype(BF16))[0:1, :]
        yg = yg + dexp_ref[:, rows] * xs[:, rows]
        yg = yg * _silu(z_ref[:, rows])
        yg = yg * lax.rsqrt(jnp.mean(yg * yg, axis=-1, keepdims=True) + EPS)
        y_ref[:, rows] = yg * ng_ref[:, rows]


def _ssd_step_call(xbc, dt, z, conv_state, ssm_state_all, p, expand, *, layer, h_stack=None):
    bs = xbc.shape[0]
    depth = ssm_state_all.shape[0]
    per_b = lambda b: (b, 0, 0)
    per_lb = lambda b: (layer, b, 0, 0)
    const = lambda b: (0, 0)
    tail = SSD_CONV - 1
    carried = () if h_stack is None else (h_stack,)
    y, h, cv = pl.pallas_call(
        functools.partial(_ssd_step_kernel, n_carried=len(carried)),
        grid=(bs,),
        in_specs=[pl.BlockSpec((None, 1, SSD_CONV_DIM), per_b), pl.BlockSpec((None, 1, DT_PAD), per_b),
                  pl.BlockSpec((None, 1, D_SSD), per_b), pl.BlockSpec((None, tail, SSD_CONV_DIM), per_b),
                  pl.BlockSpec((None, None, D_SSD, SSD_STATE), per_lb),
                  pl.BlockSpec((SSD_CONV, SSD_CONV_DIM), const), pl.BlockSpec((1, SSD_CONV_DIM), const),
                  pl.BlockSpec((1, DT_PAD), const), pl.BlockSpec((1, D_SSD), const),
                  pl.BlockSpec((1, D_SSD), const), pl.BlockSpec((1, D_SSD), const),
                  pl.BlockSpec((DT_PAD, D_SSD), const)]
                 + [pl.BlockSpec(memory_space=pl.ANY)] * len(carried),
        out_specs=[pl.BlockSpec((None, 1, D_SSD), per_b), pl.BlockSpec((None, None, D_SSD, SSD_STATE), per_lb),
                   pl.BlockSpec((None, tail, SSD_CONV_DIM), per_b)],
        out_shape=[jax.ShapeDtypeStruct((bs, 1, D_SSD), F32),
                   jax.ShapeDtypeStruct((depth, bs, D_SSD, SSD_STATE), F32),
                   jax.ShapeDtypeStruct((bs, tail, SSD_CONV_DIM), F32)],
        scratch_shapes=[pltpu.VMEM((LANES, D_SSD), F32)],
        input_output_aliases={12: 1} if carried else {},
        compiler_params=_cparams("arbitrary"),
        name="ssd_step",
    )(xbc.reshape(bs, 1, SSD_CONV_DIM), dt.reshape(bs, 1, DT_PAD), z.reshape(bs, 1, D_SSD), conv_state,
      ssm_state_all.reshape(depth, bs, D_SSD, SSD_STATE), p["ssd_conv_w"], p["ssd_conv_b"], p["dtb_pad"],
      p["alog_exp"], p["d_exp"], p["ssd_norm_g"], expand, *carried)
    return y.reshape(bs, D_SSD), h, cv


def _conf_tail(u, gc, lg_ref, lb_ref, pw_ref):
    mu = jnp.mean(u, axis=-1, keepdims=True)
    uc = u - mu
    var = jnp.mean(uc * uc, axis=-1, keepdims=True)
    un = _silu(uc * lax.rsqrt(var + EPS) * lg_ref[...] + lb_ref[...])
    return _dot(un.astype(BF16), pw_ref[...]) * _silu(gc)


def _conf_kernel(glu_ref, w_ref, b_ref, lg_ref, lb_ref, pw_ref, y_ref, bufout_ref, ub_ref, cv_ref, sh_ref,
                 *, tb, rb):
    c = pl.program_id(1)
    hist = CONF_WIDTH - 1
    pad = CONF_PAD
    sub = 8

    @pl.when(c == 0)
    def _():
        ub_ref[0:pad, :] = jnp.zeros((pad, D_CONF), F32)

    ub_ref[pad:pad + tb, :] = glu_ref[:, 0:D_CONF] * jax.nn.sigmoid(glu_ref[:, D_CONF:2 * D_CONF])
    n_sh = pad + tb - sub
    for s in range(1, sub):
        sh_ref[s, 0:n_sh, :] = ub_ref[s:s + n_sh, :]
    for r in range(tb // rb):
        acc = jnp.broadcast_to(b_ref[...], (rb, D_CONF))
        for k in range(CONF_WIDTH):
            off = pad - hist + k
            s = off % sub
            lo = off - s + r * rb
            src = ub_ref[lo:lo + rb, :] if s == 0 else sh_ref[s, lo:lo + rb, :]
            acc = acc + w_ref[k:k + 1, :] * src
        cv_ref[r * rb:(r + 1) * rb, :] = acc
    new_hist = ub_ref[pad + tb - hist:pad + tb, :]
    bufout_ref[...] = new_hist
    ub_ref[pad - hist:pad, :] = new_hist
    y_ref[...] = _conf_tail(cv_ref[...], glu_ref[:, 2 * D_CONF:3 * D_CONF], lg_ref, lb_ref, pw_ref)


def _conf_call(glu, p, *, nb, t):
    tb = min(256, t)
    nt = t // tb
    hist = CONF_WIDTH - 1
    row = lambda b, c: (b * nt + c, 0)
    const = lambda b, c: (0, 0)
    return pl.pallas_call(
        functools.partial(_conf_kernel, tb=tb, rb=32),
        grid=(nb, nt),
        in_specs=[pl.BlockSpec((tb, 3 * D_CONF), row),
                  pl.BlockSpec((CONF_WIDTH, D_CONF), const), pl.BlockSpec((1, D_CONF), const),
                  pl.BlockSpec((1, D_CONF), const), pl.BlockSpec((1, D_CONF), const),
                  pl.BlockSpec((D_CONF, D_CONF), const)],
        out_specs=[pl.BlockSpec((tb, D_CONF), row),
                   pl.BlockSpec((None, hist, D_CONF), lambda b, c: (b, 0, 0))],
        out_shape=[jax.ShapeDtypeStruct((nb * t, D_CONF), F32),
                   jax.ShapeDtypeStruct((nb, hist, D_CONF), F32)],
        scratch_shapes=[pltpu.VMEM((CONF_PAD + tb, D_CONF), F32), pltpu.VMEM((tb, D_CONF), F32),
                        pltpu.VMEM((8, CONF_PAD + tb, D_CONF), F32)],
        compiler_params=_cparams("arbitrary", "arbitrary"),
        name="conformer_prompt",
    )(glu, p["conf_conv_w"], p["conf_conv_b"], p["conf_ln_g"], p["conf_ln_b"], p["pw_bf16"])


def _conf_step_kernel(glu_ref, buf_ref, w_ref, b_ref, lg_ref, lb_ref, pw_ref, y_ref, bufout_ref):
    hist = CONF_WIDTH - 1
    u = glu_ref[:, 0:D_CONF] * jax.nn.sigmoid(glu_ref[:, D_CONF:2 * D_CONF])
    acc = b_ref[...] + w_ref[hist:hist + 1, :] * u
    for k in range(hist):
        acc = acc + w_ref[k:k + 1, :] * buf_ref[k]
    for k in range(hist - 1):
        bufout_ref[k] = buf_ref[k + 1]
    bufout_ref[hist - 1] = u
    y_ref[...] = _conf_tail(acc, glu_ref[:, 2 * D_CONF:3 * D_CONF], lg_ref, lb_ref, pw_ref)


def _conf_step_call(glu, buf_tap_major, p):
    bs = glu.shape[0]
    hist = CONF_WIDTH - 1
    return pl.pallas_call(
        _conf_step_kernel,
        out_shape=[jax.ShapeDtypeStruct((bs, D_CONF), F32),
                   jax.ShapeDtypeStruct((hist, bs, D_CONF), F32)],
        compiler_params=pltpu.CompilerParams(vmem_limit_bytes=VMEM_LIMIT),
        name="conformer_step",
    )(glu, buf_tap_major, p["conf_conv_w"], p["conf_conv_b"], p["conf_ln_g"], p["conf_ln_b"], p["pw_bf16"])


def _att_kernel(bias_ref, q_ref, kb_ref, vb_ref, ga_ref, o_ref, acc_ref, nrsp_ref, z0_ref, z1_ref, w0_ref, w1_ref,
                *, tq, hps):
    grp = pl.program_id(1)
    qi = pl.program_id(2)
    hd = ATT_HEAD_DIM

    q = (q_ref[...] * (hd ** -0.5 * LOG2E)).astype(BF16)
    row = lax.broadcasted_iota(jnp.int32, (tq, tq), 0)
    col = lax.broadcasted_iota(jnp.int32, (tq, tq), 1)
    causal = col < row
    later = (row > col).astype(BF16)
    lane = lax.broadcasted_iota(jnp.int32, (tq, hd), 1)
    ones_rows = jnp.ones((hd, tq), BF16)
    qhs = []
    for hh in range(hps):
        parts = _split3(jnp.full((tq, hd), bias_ref[grp * hps + hh] * LOG2E, F32))
        extra = jnp.zeros((tq, hd), F32)
        for idx, part in enumerate(parts):
            extra = jnp.where(lane == idx, part.astype(F32), extra)
        qhs.append(jnp.concatenate([q[:, hh * hd:(hh + 1) * hd], extra.astype(BF16)], axis=1))

    heads = range(hps)
    rows = [slice(hh * hd, (hh + 1) * hd) for hh in heads]

    z_refs = (z0_ref, z1_ref)
    w_refs = (w0_ref, w1_ref)

    def logits(hh, j, slot):
        z_refs[slot][hh] = _dot(qhs[hh], jnp.concatenate([kb_ref[j, rows[hh], :], ones_rows], axis=0))

    def accumulate(hh, j, slot):
        acc_ref[hh] = acc_ref[hh] + _dot_nt(w_refs[slot][hh], vb_ref[j, rows[hh], :])

    def weights(hh, masked, slot):
        for r in range(tq // ATT_ROW_BLOCK):
            rs = slice(r * ATT_ROW_BLOCK, (r + 1) * ATT_ROW_BLOCK)
            z = z_refs[slot][hh, rs, :]
            neg_abs = lax.bitcast_convert_type(lax.bitcast_convert_type(z, jnp.uint32) | jnp.uint32(0x80000000), F32)
            sp = jnp.maximum(z, 0.0) + jnp.log(1.0 + jnp.exp2(neg_abs)) * LOG2E
            nrsp = nrsp_ref[hh, rs, :]
            lsig = z - sp + nrsp
            if masked:
                sp = jnp.where(causal[rs, :], sp, 0.0)
            csp = _dot(sp.astype(BF16), later)
            w = jnp.exp2(lsig - csp)
            if masked:
                w = jnp.where(causal[rs, :], w, 0.0)
            w_refs[slot][hh, rs, :] = w.astype(BF16)
            nrsp_ref[hh, rs, :] = nrsp - (csp[:, 0:1] + sp[:, 0:1])

    acc_ref[...] = jnp.zeros_like(acc_ref)
    nrsp_ref[...] = jnp.zeros_like(nrsp_ref)
    for hh in heads:
        logits(hh, qi, 0)
    for hh in heads:
        weights(hh, True, 0)
        logits(hh, jnp.maximum(qi - 1, 0), 1)

    def step(j, slot):
        for hh in heads:
            accumulate(hh, j + 1, 1 - slot)
            logits(hh, jnp.maximum(j - 1, 0), 1 - slot)
            weights(hh, False, slot)

    @pl.loop(0, qi // 2)
    def _(it):
        j = qi - 1 - 2 * it
        step(j, 1)
        step(j - 1, 0)

    @pl.when(qi % 2 == 1)
    def _():
        step(0, 1)

    for parity in range(2):
        @pl.when(qi % 2 == parity)
        def _():
            for hh in heads:
                accumulate(hh, 0, parity)
    o = jnp.concatenate([acc_ref[hh] for hh in heads], axis=1)
    o_ref[...] = o * _silu(ga_ref[...])


def _att_call(q, kb, vb, ga, bias, *, nb, t, tq):
    nq = t // tq
    hps = ATT_HEADS_PER_STEP
    width = hps * ATT_HEAD_DIM
    qspec = pl.BlockSpec((tq, width), lambda b, g, i: (b * nq + i, g))
    kspec = pl.BlockSpec((None, nq, width, tq), lambda b, g, i: (b, 0, g, 0))
    return pl.pallas_call(
        functools.partial(_att_kernel, tq=tq, hps=hps),
        grid=(nb, N_ATT_HEADS // hps, nq),
        in_specs=[pl.BlockSpec(memory_space=pltpu.SMEM), qspec, kspec, kspec, qspec],
        out_specs=qspec,
        out_shape=jax.ShapeDtypeStruct((nb * t, D_ATT), F32),
        scratch_shapes=[pltpu.VMEM((hps, tq, ATT_HEAD_DIM), F32), pltpu.VMEM((hps, tq, 1), F32),
                        pltpu.VMEM((hps, tq, tq), F32), pltpu.VMEM((hps, tq, tq), F32),
                        pltpu.VMEM((hps, tq, tq), BF16), pltpu.VMEM((hps, tq, tq), BF16)],
        compiler_params=_cparams("arbitrary", "arbitrary", "arbitrary"),
        name="stickbreak_prompt",
    )(bias, q, kb, vb, ga)


def _att_dec_kernel(pt_ref, bias_ref, q_ref, ga_ref, *refs, npg):
    del pt_ref
    k_refs = refs[:npg]
    v_refs = refs[npg:2 * npg]
    o_ref, acc_ref, rsp_ref = refs[2 * npg:]
    j = pl.program_id(1)
    nh = N_ATT_HEADS

    @pl.when(j == 0)
    def _():
        acc_ref[...] = jnp.zeros_like(acc_ref)
        rsp_ref[...] = jnp.zeros_like(rsp_ref)

    head = lax.broadcasted_iota(jnp.int32, (nh, D_ATT), 0)
    lane = lax.broadcasted_iota(jnp.int32, (nh, D_ATT), 1)
    own = lane // ATT_HEAD_DIM == head
    q = q_ref[...] * (ATT_HEAD_DIM ** -0.5)
    qbd = jnp.where(own, jnp.broadcast_to(q, (nh, D_ATT)), 0.0).astype(BF16)
    bias = bias_ref[...]
    zs = [_dot(qbd, k_refs[i][...].astype(BF16)) + bias for i in range(npg)]
    sps = [_softplus(z) for z in zs]
    row = lax.broadcasted_iota(jnp.int32, (PAGE_SIZE, PAGE_SIZE), 0)
    col = lax.broadcasted_iota(jnp.int32, (PAGE_SIZE, PAGE_SIZE), 1)
    later = (row > col).astype(BF16)
    loc = _dot_exact_rhs(jnp.concatenate(sps, axis=0), later)
    run = rsp_ref[...]
    acc = acc_ref[...]
    for i in reversed(range(npg)):
        w = jnp.exp(zs[i] - sps[i] - loc[i * nh:(i + 1) * nh, :] - run)
        acc = acc + _dot_nt(w.astype(BF16), v_refs[i][...].astype(BF16))
        run = run + jnp.sum(sps[i], axis=-1, keepdims=True)
    acc_ref[...] = acc
    rsp_ref[...] = run

    @pl.when(j == pl.num_programs(1) - 1)
    def _():
        o = jnp.sum(jnp.where(own, acc, 0.0), axis=0, keepdims=True)
        o_ref[...] = o * _silu(ga_ref[...])


def _att_dec_call(q, ga, cache_k, cache_v, page_table, bias, layer):
    bs, n_pages = page_table.shape
    npg = min(PAGES_PER_STEP, n_pages)
    ng = n_pages // npg
    depth, n_pool = cache_k.shape[:2]
    ck = jnp.transpose(cache_k, (0, 1, 3, 4, 2)).reshape(depth, n_pool, D_ATT, PAGE_SIZE)
    cv = jnp.transpose(cache_v, (0, 1, 3, 4, 2)).reshape(depth, n_pool, D_ATT, PAGE_SIZE)

    def page_spec(i):
        return pl.BlockSpec((None, None, D_ATT, PAGE_SIZE),
                            lambda b, j, pt: (layer, pt[b, (ng - 1 - j) * npg + i], 0, 0))

    per_b = lambda b, j, pt: (b, 0, 0)
    grid_spec = pltpu.PrefetchScalarGridSpec(
        num_scalar_prefetch=1,
        grid=(bs, ng),
        in_specs=[pl.BlockSpec((N_ATT_HEADS, 1), lambda b, j, pt: (0, 0)),
                  pl.BlockSpec((None, 1, D_ATT), per_b), pl.BlockSpec((None, 1, D_ATT), per_b)]
                 + [page_spec(i) for i in range(npg)] * 2,
        out_specs=pl.BlockSpec((None, 1, D_ATT), per_b),
        scratch_shapes=[pltpu.VMEM((N_ATT_HEADS, D_ATT), F32), pltpu.VMEM((N_ATT_HEADS, 1), F32)],
    )
    out = pl.pallas_call(
        functools.partial(_att_dec_kernel, npg=npg),
        grid_spec=grid_spec,
        out_shape=jax.ShapeDtypeStruct((bs, 1, D_ATT), F32),
        compiler_params=_cparams("arbitrary", "arbitrary"),
        name="stickbreak_paged",
    )(page_table, bias.reshape(N_ATT_HEADS, 1), q.reshape(bs, 1, D_ATT), ga.reshape(bs, 1, D_ATT),
      *([ck] * npg), *([cv] * npg))
    return out.reshape(bs, D_ATT)


def _outproj_kernel(ys_ref, yc_ref, ya_ref, x_ref, gate_ref, w_ref, fg_ref, o_ref, *, final):
    acc = _dot(ys_ref[...].astype(BF16), w_ref[0:D_SSD, :])
    acc = acc + _dot(yc_ref[...].astype(BF16), w_ref[D_SSD:D_SSD + D_CONF, :])
    acc = acc + _dot(ya_ref[...].astype(BF16), w_ref[D_SSD + D_CONF:, :])
    xn = x_ref[...] + gate_ref[...] * acc
    if final:
        xn = xn * lax.rsqrt(jnp.mean(xn * xn, axis=-1, keepdims=True) + EPS) * fg_ref[...]
    o_ref[...] = xn


def _outproj_call(ys, yc, ya, x, mod, w_bf16, final_g, *, tm, rows_per_batch, final):
    n, d = x.shape
    if rows_per_batch > 1:
        per = rows_per_batch // tm
        gate_spec = pl.BlockSpec((None, 1, d), lambda i: (i // per, 0, 2))
    else:
        gate_spec = pl.BlockSpec((tm, d), lambda i: (i, 2))
    row = lambda i: (i, 0)
    return pl.pallas_call(
        functools.partial(_outproj_kernel, final=final),
        grid=(n // tm,),
        in_specs=[pl.BlockSpec((tm, D_SSD), row), pl.BlockSpec((tm, D_CONF), row), pl.BlockSpec((tm, D_ATT), row),
                  pl.BlockSpec((tm, d), row), gate_spec,
                  pl.BlockSpec((D_SSD + D_CONF + D_ATT, d), lambda i: (0, 0)),
                  pl.BlockSpec((1, d), lambda i: (0, 0))],
        out_specs=pl.BlockSpec((tm, d), row),
        out_shape=jax.ShapeDtypeStruct((n, d), F32),
        compiler_params=_cparams("arbitrary"),
        name="outproj_residual",
    )(ys, yc, ya, x, mod, w_bf16, final_g.reshape(1, d))


def _prep_layer(l, w_in, ssd_conv_w, ssd_conv_b, ssd_dt_bias, ssd_a_log, ssd_d, ssd_norm_g, conf_conv_w,
                conf_conv_b, conf_ln_g, conf_ln_b, conf_w_pw, w_out):
    wt = jnp.swapaxes(w_in, 1, 2)[l]
    pad16 = lambda v: jnp.pad(v, (0, DT_PAD - N_SSD_HEADS)).reshape(1, DT_PAD)
    rep = lambda v: jnp.repeat(v, SSD_HEAD_DIM).reshape(1, D_SSD)
    return dict(
        w_in=wt.astype(BF16),
        ssd_conv_w=ssd_conv_w[l], ssd_conv_b=ssd_conv_b[l].reshape(1, -1),
        dtb_pad=pad16(ssd_dt_bias[l]), alog_pad=pad16(ssd_a_log[l]), alog_exp=rep(ssd_a_log[l]),
        d_exp=rep(ssd_d[l]), ssd_norm_g=ssd_norm_g[l].reshape(1, -1),
        conf_conv_w=conf_conv_w[l], conf_conv_b=conf_conv_b[l].reshape(1, -1),
        conf_ln_g=conf_ln_g[l].reshape(1, -1), conf_ln_b=conf_ln_b[l].reshape(1, -1),
        pw_bf16=conf_w_pw[l].astype(BF16), w_out=w_out[l].astype(BF16))


def kernel(x_prompt, x_sample, c_prompt, c_sample, cache_k, cache_v, page_table, state_ssm, state_ssd_conv, state_conf_conv, norm_g, w_ada, b_ada, w_in, ssd_conv_w, ssd_conv_b, ssd_dt_bias, ssd_a_log, ssd_d, ssd_norm_g, conf_conv_w, conf_conv_b, conf_ln_g, conf_ln_b, conf_w_pw, att_logit_bias, w_out, final_norm_g):
    nb, t, d = x_prompt.shape
    bs = x_sample.shape[0]
    depth = w_in.shape[0]

    n_mod = nb + bs
    n_mod_pad = -(-n_mod // 8) * 8
    c_all = jnp.concatenate([c_prompt, c_sample, jnp.zeros((n_mod_pad - n_mod, d), F32)], axis=0)
    mod = _ada_call(c_all, w_ada, b_ada)

    lane_head = jnp.arange(D_SSD) // SSD_HEAD_DIM
    expand = (jnp.arange(DT_PAD)[:, None] == lane_head[None, :]).astype(BF16)

    xp = x_prompt.reshape(nb * t, d)
    xs = x_sample.reshape(bs, d)
    tm_p = min(ATT_TILE, t)
    outs_p = []
    outs_s = []
    kv_stack = None
    h_s_all = None
    for l in range(depth):
        p = _prep_layer(l, w_in, ssd_conv_w, ssd_conv_b, ssd_dt_bias, ssd_a_log, ssd_d, ssd_norm_g,
                        conf_conv_w, conf_conv_b, conf_ln_g, conf_ln_b, conf_w_pw, w_out)
        final = l == depth - 1
        mod_p = mod[l, :nb].reshape(nb, 1, 3 * d)
        mod_s = mod[l, nb:nb + bs]

        z, xbc, glu, q, kt_all, vt_all, ga, dt, kb, vb = _inproj_call(
            xp, mod_p, norm_g[l], p["w_in"], tm=tm_p, rows_per_batch=t, layer=l, depth=depth, kv_stack=kv_stack)
        kv_stack = (kt_all, vt_all)
        y_ssd, h_p, cv_p = _ssd_call(xbc, dt, z, p, expand, nb=nb, t=t)
        y_conf, cf_p = _conf_call(glu, p, nb=nb, t=t)
        y_att = _att_call(q, kb, vb, ga, att_logit_bias[l], nb=nb, t=t, tq=tm_p)
        xp = _outproj_call(y_ssd, y_conf, y_att, xp, mod_p, p["w_out"], final_norm_g, tm=tm_p,
                           rows_per_batch=t, final=final)
        outs_p.append((h_p.reshape(nb, N_SSD_HEADS, SSD_HEAD_DIM, SSD_STATE), cv_p, cf_p))

        z, xbc, glu, q, k, v, ga, dt = _inproj_call(xs, mod_s, norm_g[l], p["w_in"], tm=bs, rows_per_batch=1)
        y_ssd, h_s_all, cv_s = _ssd_step_call(xbc, dt, z, state_ssd_conv[l], state_ssm, p, expand, layer=l,
                                              h_stack=h_s_all)
        y_conf, cf_s = _conf_step_call(glu, jnp.swapaxes(state_conf_conv[l], 0, 1), p)
        y_att = _att_dec_call(q, ga, cache_k, cache_v, page_table, att_logit_bias[l], l)
        xs = _outproj_call(y_ssd, y_conf, y_att, xs, mod_s, p["w_out"], final_norm_g, tm=bs,
                           rows_per_batch=1, final=final)
        outs_s.append((k.reshape(bs, 1, N_ATT_HEADS, ATT_HEAD_DIM), v.reshape(bs, 1, N_ATT_HEADS, ATT_HEAD_DIM),
                       cv_s, jnp.swapaxes(cf_s, 0, 1)))

    stack = lambda outs, i: jnp.stack([o[i] for o in outs])
    heads_last = lambda a: jnp.transpose(a.reshape(depth, nb, N_ATT_HEADS, ATT_HEAD_DIM, t), (0, 1, 4, 2, 3))
    return (xp.reshape(nb, t, d), xs.reshape(bs, 1, d),
            heads_last(kv_stack[0]), heads_last(kv_stack[1]), stack(outs_s, 0), stack(outs_s, 1),
            stack(outs_p, 0), h_s_all.reshape(depth, bs, N_SSD_HEADS, SSD_HEAD_DIM, SSD_STATE),
            stack(outs_p, 1), stack(outs_s, 2), stack(outs_p, 2), stack(outs_s, 3))
```

```python
import functools

import jax
import jax.numpy as jnp
from jax import lax
from jax.experimental import pallas as pl
from jax.experimental.pallas import tpu as pltpu

F32 = jnp.float32
BF16 = jnp.bfloat16
EPS = 1e-6
ACT_DTYPE = BF16
LOG2E = 1.4426950408889634

D_MODEL = 1024
D_SSD = 1024
SSD_HEAD_DIM = 64
N_SSD_HEADS = 16
SSD_GROUPS = 2
SSD_STATE = 128
SSD_CONV = 4
SSD_CONV_DIM = D_SSD + 2 * SSD_GROUPS * SSD_STATE
D_CONF = 512
CONF_WIDTH = 31
D_ATT = 512
N_ATT_HEADS = 8
ATT_HEAD_DIM = 64
PAGE_SIZE = 128

LANES = 128
SSD_CHUNK = 128
ATT_TILE = 256
ATT_HEADS_PER_STEP = 4
OUT_TILE = 512
CONF_PAD = 32
PAGES_PER_STEP = 32
ATT_ROW_BLOCK = 128
DT_PAD = LANES
PROJ_COLS = (("z", D_SSD), ("xbc", SSD_CONV_DIM), ("glu", 3 * D_CONF), ("q", D_ATT), ("k", D_ATT),
             ("v", D_ATT), ("ga", D_ATT), ("dt", DT_PAD))
_DT_ROW = D_SSD + SSD_CONV_DIM
_GLU_ROW = _DT_ROW + N_SSD_HEADS
PROJ_ROWS = dict(z=0, xbc=D_SSD, dt=_DT_ROW, glu=_GLU_ROW, q=_GLU_ROW + 3 * D_CONF,
                 k=_GLU_ROW + 3 * D_CONF + D_ATT, v=_GLU_ROW + 3 * D_CONF + 2 * D_ATT,
                 ga=_GLU_ROW + 3 * D_CONF + 3 * D_ATT)
D_PROJ = _GLU_ROW + 3 * D_CONF + 4 * D_ATT
VMEM_LIMIT = 56 * 1024 * 1024


def _cparams(*sem):
    return pltpu.CompilerParams(dimension_semantics=sem, vmem_limit_bytes=VMEM_LIMIT)


def _silu(x):
    return x * jax.nn.sigmoid(x)


def _softplus(x):
    return jnp.maximum(x, 0.0) + jnp.log(1.0 + jnp.exp(-jnp.abs(x)))


def _split3(v):
    hi = v.astype(BF16)
    r = v - hi.astype(F32)
    mid = r.astype(BF16)
    lo = (r - mid.astype(F32)).astype(BF16)
    return hi, mid, lo


def _dot(a, b):
    return jnp.dot(a, b, preferred_element_type=F32)


def _dot_nt(a, b):
    return lax.dot_general(a, b, (((1,), (1,)), ((), ())), preferred_element_type=F32)


def _dot_exact_rhs(v, m):
    hi, mid, lo = _split3(v)
    return _dot(hi, m) + _dot(mid, m) + _dot(lo, m)


def _dot_exact_lhs(m, v):
    hi, mid, lo = _split3(v)
    return _dot(m, hi) + _dot(m, mid) + _dot(m, lo)


def _ada_kernel(c_ref, w_ref, b_ref, o_ref):
    s = _silu(c_ref[...]).astype(BF16)
    o_ref[...] = _dot(s, w_ref[...].astype(BF16)) + b_ref[...]


def _ada_call(c_all, w_ada, b_ada):
    depth, d, d3 = w_ada.shape
    rows = c_all.shape[0]
    tn = 1024
    return pl.pallas_call(
        _ada_kernel,
        grid=(depth, d3 // tn),
        in_specs=[pl.BlockSpec((rows, d), lambda l, j: (0, 0)),
                  pl.BlockSpec((None, d, tn), lambda l, j: (l, 0, j)),
                  pl.BlockSpec((None, 1, tn), lambda l, j: (l, 0, j))],
        out_specs=pl.BlockSpec((None, rows, tn), lambda l, j: (l, 0, j)),
        out_shape=jax.ShapeDtypeStruct((depth, rows, d3), F32),
        compiler_params=_cparams("arbitrary", "arbitrary"),
        name="ada_mod",
    )(c_all, w_ada, b_ada.reshape(depth, 1, d3))


def _inproj_kernel(x_ref, shift_ref, scale_ref, g_ref, wt_ref, *refs, kv_transposed, n_carried):
    out_refs = refs[n_carried:]
    x = x_ref[...]
    xn = x * lax.rsqrt(jnp.mean(x * x, axis=-1, keepdims=True) + EPS) * g_ref[...]
    h = (xn * (1.0 + scale_ref[...]) + shift_ref[...]).astype(BF16)
    bf16_refs = dict(zip(("k", "v"), out_refs[len(PROJ_COLS):]))
    for (name, width), o_ref in zip(PROJ_COLS, out_refs):
        w_rows = wt_ref[PROJ_ROWS[name]:PROJ_ROWS[name] + width, :]
        if kv_transposed and name in ("k", "v"):
            r = _dot_nt(w_rows, h)
            o_ref[...] = r
            bf16_refs[name][...] = r.astype(BF16)
        else:
            o_ref[...] = _dot_nt(h, w_rows)


def _inproj_call(x, mod, norm_g, wt_bf16, *, tm, rows_per_batch, layer=0, depth=1, kv_stack=None):
    n, d = x.shape
    kv_transposed = rows_per_batch > 1
    out_specs = [pl.BlockSpec((tm, w), lambda i: (i, 0)) for _, w in PROJ_COLS]
    out_shape = [jax.ShapeDtypeStruct((n, w), F32) for _, w in PROJ_COLS]
    carried = ()
    aliases = {}
    if kv_transposed:
        per = rows_per_batch // tm
        nb = n // rows_per_batch
        shift_spec = pl.BlockSpec((None, 1, d), lambda i: (i // per, 0, 0))
        scale_spec = pl.BlockSpec((None, 1, d), lambda i: (i // per, 0, 1))
        for idx, (name, w) in enumerate(PROJ_COLS):
            if name in ("k", "v"):
                out_specs[idx] = pl.BlockSpec((None, None, w, tm), lambda i: (layer, i // per, 0, i % per))
                out_shape[idx] = jax.ShapeDtypeStruct((depth, nb, w, rows_per_batch), F32)
        for _ in ("k", "v"):
            out_specs.append(pl.BlockSpec((None, None, D_ATT, tm), lambda i: (i // per, i % per, 0, 0)))
            out_shape.append(jax.ShapeDtypeStruct((nb, per, D_ATT, tm), BF16))
        if kv_stack is not None:
            carried = tuple(kv_stack)
            names = [name for name, _ in PROJ_COLS]
            aliases = {5: names.index("k"), 6: names.index("v")}
    else:
        shift_spec = pl.BlockSpec((tm, d), lambda i: (i, 0))
        scale_spec = pl.BlockSpec((tm, d), lambda i: (i, 1))
    return pl.pallas_call(
        functools.partial(_inproj_kernel, kv_transposed=kv_transposed, n_carried=len(carried)),
        grid=(n // tm,),
        in_specs=[pl.BlockSpec((tm, d), lambda i: (i, 0)), shift_spec, scale_spec,
                  pl.BlockSpec((1, d), lambda i: (0, 0)),
                  pl.BlockSpec((D_PROJ, d), lambda i: (0, 0), pipeline_mode=pl.Buffered(1))]
                 + [pl.BlockSpec(memory_space=pl.ANY)] * len(carried),
        out_specs=out_specs,
        out_shape=out_shape,
        input_output_aliases=aliases,
        compiler_params=_cparams("arbitrary"),
        name="norm_inproj",
    )(x, mod, mod, norm_g.reshape(1, d), wt_bf16, *carried)


def _ssd_kernel(xbc_ref, dt_ref, z_ref, cw_ref, cb_ref, dtb_ref, alog_ref, dexp_ref, ng_ref, e_ref,
                y_ref, hout_ref, convout_ref, xp_ref, ht_ref, yacc_ref):
    c = pl.program_id(1)
    L = SSD_CHUNK
    tail = SSD_CONV - 1
    base = 8 - tail

    @pl.when(c == 0)
    def _():
        ht_ref[...] = jnp.zeros_like(ht_ref)
        xp_ref[0:8, :] = jnp.zeros((8, SSD_CONV_DIM), F32)

    xp_ref[8:8 + L, :] = xbc_ref[...]
    acc = jnp.broadcast_to(cb_ref[...], (L, SSD_CONV_DIM))
    for k in range(SSD_CONV):
        acc = acc + cw_ref[k:k + 1, :] * xp_ref[base + k:base + k + L, :]
    new_tail = xp_ref[8 + L - tail:8 + L, :]
    convout_ref[...] = new_tail
    xp_ref[base:8, :] = new_tail
    xa = _silu(acc)
    xs = xa[:, :D_SSD]

    dt = _softplus(dt_ref[...] + dtb_ref[...])
    a = -jnp.exp(alog_ref[...])
    da = dt * a
    row = lax.broadcasted_iota(jnp.int32, (L, L), 0)
    col = lax.broadcasted_iota(jnp.int32, (L, L), 1)
    causal = col <= row
    tril = causal.astype(BF16)
    a_cs = _dot_exact_lhs(tril, da)
    a_cs_t = a_cs.T
    dt_t = dt.T
    a_last = a_cs[L - 1:L, :]
    expa_x = _dot_exact_rhs(jnp.exp(a_cs), e_ref[...])
    wgt_x = _dot_exact_rhs(dt * jnp.exp(a_last - a_cs), e_ref[...])
    cdec_x = expa_x[L - 1:L, :]

    hpg = N_SSD_HEADS // SSD_GROUPS
    gw = D_SSD // SSD_GROUPS
    pair_lane = lax.broadcasted_iota(jnp.int32, (L, 2 * SSD_HEAD_DIM), 1)
    for g in range(SSD_GROUPS):
        gs = slice(g * gw, (g + 1) * gw)
        bg = xa[:, D_SSD + g * SSD_STATE:D_SSD + (g + 1) * SSD_STATE]
        cg = xa[:, D_SSD + (SSD_GROUPS + g) * SSD_STATE:D_SSD + (SSD_GROUPS + g + 1) * SSD_STATE]
        bgb = bg.astype(BF16)
        cgb = cg.astype(BF16)
        bgt = bg.T.astype(BF16)
        cbm = _dot_nt(cgb, bgb)
        htg = ht_ref[:, gs]
        y_off = _dot(cgb, htg.astype(BF16)) * expa_x[:, gs]
        xw = (xs[:, gs] * wgt_x[:, gs]).astype(BF16)
        ht_ref[:, gs] = cdec_x[:, gs] * htg + _dot(bgt, xw)
        for pr in range(hpg // 2):
            ms = []
            for e in (g * hpg + 2 * pr, g * hpg + 2 * pr + 1):
                seg = a_cs[:, e:e + 1] - a_cs_t[e:e + 1, :]
                dec = jnp.exp(jnp.where(causal, seg, -jnp.inf))
                ms.append((cbm * dec * dt_t[e:e + 1, :]).astype(BF16))
            ps = slice(g * gw + pr * 2 * SSD_HEAD_DIM, g * gw + (pr + 1) * 2 * SSD_HEAD_DIM)
            x_pair = xs[:, ps]
            x_bd = jnp.concatenate([jnp.where(pair_lane < SSD_HEAD_DIM, x_pair, 0.0),
                                    jnp.where(pair_lane >= SSD_HEAD_DIM, x_pair, 0.0)], axis=0).astype(BF16)
            y_diag = _dot(jnp.concatenate(ms, axis=1), x_bd)
            yacc_ref[:, ps] = y_diag + y_off[:, pr * 2 * SSD_HEAD_DIM:(pr + 1) * 2 * SSD_HEAD_DIM]

    y = yacc_ref[...] + dexp_ref[...] * xs
    y = y * _silu(z_ref[...])
    gw = D_SSD // SSD_GROUPS
    for g in range(SSD_GROUPS):
        yg = y[:, g * gw:(g + 1) * gw]
        yg = yg * lax.rsqrt(jnp.mean(yg * yg, axis=-1, keepdims=True) + EPS)
        y_ref[:, g * gw:(g + 1) * gw] = (yg * ng_ref[:, g * gw:(g + 1) * gw]).astype(y_ref.dtype)

    @pl.when(c == pl.num_programs(1) - 1)
    def _():
        hout_ref[...] = ht_ref[...].T


def _ssd_call(xbc, dt, z, p, expand, *, nb, t):
    nc = t // SSD_CHUNK
    L = SSD_CHUNK
    row = lambda b, c: (b * nc + c, 0)
    const = lambda b, c: (0, 0)
    return pl.pallas_call(
        _ssd_kernel,
        grid=(nb, nc),
        in_specs=[pl.BlockSpec((L, SSD_CONV_DIM), row), pl.BlockSpec((L, DT_PAD), row),
                  pl.BlockSpec((L, D_SSD), row),
                  pl.BlockSpec((SSD_CONV, SSD_CONV_DIM), const), pl.BlockSpec((1, SSD_CONV_DIM), const),
                  pl.BlockSpec((1, DT_PAD), const), pl.BlockSpec((1, DT_PAD), const),
                  pl.BlockSpec((1, D_SSD), const), pl.BlockSpec((1, D_SSD), const),
                  pl.BlockSpec((DT_PAD, D_SSD), const)],
        out_specs=[pl.BlockSpec((L, D_SSD), row),
                   pl.BlockSpec((None, D_SSD, SSD_STATE), lambda b, c: (b, 0, 0)),
                   pl.BlockSpec((None, SSD_CONV - 1, SSD_CONV_DIM), lambda b, c: (b, 0, 0))],
        out_shape=[jax.ShapeDtypeStruct((nb * t, D_SSD), ACT_DTYPE),
                   jax.ShapeDtypeStruct((nb, D_SSD, SSD_STATE), F32),
                   jax.ShapeDtypeStruct((nb, SSD_CONV - 1, SSD_CONV_DIM), F32)],
        scratch_shapes=[pltpu.VMEM((8 + L, SSD_CONV_DIM), F32),
                        pltpu.VMEM((SSD_STATE, D_SSD), F32),
                        pltpu.VMEM((L, D_SSD), F32)],
        compiler_params=_cparams("arbitrary", "arbitrary"),
        name="ssd_prompt",
    )(xbc, dt, z, p["ssd_conv_w"], p["ssd_conv_b"], p["dtb_pad"], p["alog_pad"], p["d_exp"], p["ssd_norm_g"],
      expand)


def _ssd_step_kernel(xbc_ref, dt_ref, z_ref, conv_ref, h_ref, cw_ref, cb_ref, dtb_ref, alog_ref, dexp_ref,
                     ng_ref, e_ref, *refs, n_carried):
    y_ref, hout_ref, convout_ref, r_ref = refs[n_carried:]
    x = xbc_ref[...]
    buf = conv_ref[...]
    tail = SSD_CONV - 1
    acc = cb_ref[...] + cw_ref[tail:tail + 1, :] * x
    for k in range(tail):
        acc = acc + cw_ref[k:k + 1, :] * buf[k:k + 1, :]
    convout_ref[0:tail - 1, :] = buf[1:tail, :]
    convout_ref[tail - 1:tail, :] = x
    xa = _silu(acc)
    xs = xa[:, :D_SSD]

    dt = _softplus(dt_ref[...] + dtb_ref[...])
    dt_e = _dot_exact_rhs(jnp.broadcast_to(dt, (8, DT_PAD)), e_ref[...])[0:1, :]
    d_a = jnp.exp(dt_e * (-jnp.exp(alog_ref[...])))
    xdt = xs * dt_e

    @pl.when(pl.program_id(0) == 0)
    def _():
        r_ref[...] = jnp.zeros_like(r_ref)

    r_ref[0:1, :] = xdt
    r_ref[1:2, :] = d_a
    rt = r_ref[...].T
    gw = D_SSD // SSD_GROUPS
    for g in range(SSD_GROUPS):
        rows = slice(g * gw, (g + 1) * gw)
        bg = xa[:, D_SSD + g * SSD_STATE:D_SSD + (g + 1) * SSD_STATE]
        cg = xa[:, D_SSD + (SSD_GROUPS + g) * SSD_STATE:D_SSD + (SSD_GROUPS + g + 1) * SSD_STATE]
        hn = rt[rows, 1:2] * h_ref[rows, :] + rt[rows, 0:1] * bg
        hout_ref[rows, :] = hn
        c8 = jnp.broadcast_to(cg, (8, SSD_STATE)).astype(BF16)
        yg = _dot_nt(c8, hn.astype(BF16))[0:1, :]
        yg = yg + dexp_ref[:, rows] * xs[:, rows]
        yg = yg * _silu(z_ref[:, rows])
        yg = yg * lax.rsqrt(jnp.mean(yg * yg, axis=-1, keepdims=True) + EPS)
        y_ref[:, rows] = yg * ng_ref[:, rows]


def _ssd_step_call(xbc, dt, z, conv_state, ssm_state_all, p, expand, *, layer, h_stack=None):
    bs = xbc.shape[0]
    depth = ssm_state_all.shape[0]
    per_b = lambda b: (b, 0, 0)
    per_lb = lambda b: (layer, b, 0, 0)
    const = lambda b: (0, 0)
    tail = SSD_CONV - 1
    carried = () if h_stack is None else (h_stack,)
    y, h, cv = pl.pallas_call(
        functools.partial(_ssd_step_kernel, n_carried=len(carried)),
        grid=(bs,),
        in_specs=[pl.BlockSpec((None, 1, SSD_CONV_DIM), per_b), pl.BlockSpec((None, 1, DT_PAD), per_b),
                  pl.BlockSpec((None, 1, D_SSD), per_b), pl.BlockSpec((None, tail, SSD_CONV_DIM), per_b),
                  pl.BlockSpec((None, None, D_SSD, SSD_STATE), per_lb),
                  pl.BlockSpec((SSD_CONV, SSD_CONV_DIM), const), pl.BlockSpec((1, SSD_CONV_DIM), const),
                  pl.BlockSpec((1, DT_PAD), const), pl.BlockSpec((1, D_SSD), const),
                  pl.BlockSpec((1, D_SSD), const), pl.BlockSpec((1, D_SSD), const),
                  pl.BlockSpec((DT_PAD, D_SSD), const)]
                 + [pl.BlockSpec(memory_space=pl.ANY)] * len(carried),
        out_specs=[pl.BlockSpec((None, 1, D_SSD), per_b), pl.BlockSpec((None, None, D_SSD, SSD_STATE), per_lb),
                   pl.BlockSpec((None, tail, SSD_CONV_DIM), per_b)],
        out_shape=[jax.ShapeDtypeStruct((bs, 1, D_SSD), F32),
                   jax.ShapeDtypeStruct((depth, bs, D_SSD, SSD_STATE), F32),
                   jax.ShapeDtypeStruct((bs, tail, SSD_CONV_DIM), F32)],
        scratch_shapes=[pltpu.VMEM((LANES, D_SSD), F32)],
        input_output_aliases={12: 1} if carried else {},
        compiler_params=_cparams("arbitrary"),
        name="ssd_step",
    )(xbc.reshape(bs, 1, SSD_CONV_DIM), dt.reshape(bs, 1, DT_PAD), z.reshape(bs, 1, D_SSD), conv_state,
      ssm_state_all.reshape(depth, bs, D_SSD, SSD_STATE), p["ssd_conv_w"], p["ssd_conv_b"], p["dtb_pad"],
      p["alog_exp"], p["d_exp"], p["ssd_norm_g"], expand, *carried)
    return y.reshape(bs, D_SSD), h, cv


def _conf_tail(u, gc, lg_ref, lb_ref, pw_ref):
    mu = jnp.mean(u, axis=-1, keepdims=True)
    uc = u - mu
    var = jnp.mean(uc * uc, axis=-1, keepdims=True)
    un = _silu(uc * lax.rsqrt(var + EPS) * lg_ref[...] + lb_ref[...])
    return _dot(un.astype(BF16), pw_ref[...]) * _silu(gc)


def _conf_kernel(glu_ref, w_ref, b_ref, lg_ref, lb_ref, pw_ref, y_ref, bufout_ref, ub_ref, cv_ref, sh_ref,
                 *, tb, rb):
    c = pl.program_id(1)
    hist = CONF_WIDTH - 1
    pad = CONF_PAD
    sub = 8

    @pl.when(c == 0)
    def _():
        ub_ref[0:pad, :] = jnp.zeros((pad, D_CONF), F32)

    ub_ref[pad:pad + tb, :] = glu_ref[:, 0:D_CONF] * jax.nn.sigmoid(glu_ref[:, D_CONF:2 * D_CONF])
    n_sh = pad + tb - sub
    for s in range(1, sub):
        sh_ref[s, 0:n_sh, :] = ub_ref[s:s + n_sh, :]
    for r in range(tb // rb):
        acc = jnp.broadcast_to(b_ref[...], (rb, D_CONF))
        for k in range(CONF_WIDTH):
            off = pad - hist + k
            s = off % sub
            lo = off - s + r * rb
            src = ub_ref[lo:lo + rb, :] if s == 0 else sh_ref[s, lo:lo + rb, :]
            acc = acc + w_ref[k:k + 1, :] * src
        cv_ref[r * rb:(r + 1) * rb, :] = acc
    new_hist = ub_ref[pad + tb - hist:pad + tb, :]
    bufout_ref[...] = new_hist
    ub_ref[pad - hist:pad, :] = new_hist
    y_ref[...] = _conf_tail(cv_ref[...], glu_ref[:, 2 * D_CONF:3 * D_CONF], lg_ref, lb_ref,
                            pw_ref).astype(y_ref.dtype)


def _conf_call(glu, p, *, nb, t):
    tb = min(256, t)
    nt = t // tb
    hist = CONF_WIDTH - 1
    row = lambda b, c: (b * nt + c, 0)
    const = lambda b, c: (0, 0)
    return pl.pallas_call(
        functools.partial(_conf_kernel, tb=tb, rb=32),
        grid=(nb, nt),
        in_specs=[pl.BlockSpec((tb, 3 * D_CONF), row),
                  pl.BlockSpec((CONF_WIDTH, D_CONF), const), pl.BlockSpec((1, D_CONF), const),
                  pl.BlockSpec((1, D_CONF), const), pl.BlockSpec((1, D_CONF), const),
                  pl.BlockSpec((D_CONF, D_CONF), const)],
        out_specs=[pl.BlockSpec((tb, D_CONF), row),
                   pl.BlockSpec((None, hist, D_CONF), lambda b, c: (b, 0, 0))],
        out_shape=[jax.ShapeDtypeStruct((nb * t, D_CONF), ACT_DTYPE),
                   jax.ShapeDtypeStruct((nb, hist, D_CONF), F32)],
        scratch_shapes=[pltpu.VMEM((CONF_PAD + tb, D_CONF), F32), pltpu.VMEM((tb, D_CONF), F32),
                        pltpu.VMEM((8, CONF_PAD + tb, D_CONF), F32)],
        compiler_params=_cparams("arbitrary", "arbitrary"),
        name="conformer_prompt",
    )(glu, p["conf_conv_w"], p["conf_conv_b"], p["conf_ln_g"], p["conf_ln_b"], p["pw_bf16"])


def _conf_step_kernel(glu_ref, buf_ref, w_ref, b_ref, lg_ref, lb_ref, pw_ref, y_ref, bufout_ref):
    hist = CONF_WIDTH - 1
    u = glu_ref[:, 0:D_CONF] * jax.nn.sigmoid(glu_ref[:, D_CONF:2 * D_CONF])
    acc = b_ref[...] + w_ref[hist:hist + 1, :] * u
    for k in range(hist):
        acc = acc + w_ref[k:k + 1, :] * buf_ref[k]
    for k in range(hist - 1):
        bufout_ref[k] = buf_ref[k + 1]
    bufout_ref[hist - 1] = u
    y_ref[...] = _conf_tail(acc, glu_ref[:, 2 * D_CONF:3 * D_CONF], lg_ref, lb_ref, pw_ref)


def _conf_step_call(glu, buf_tap_major, p):
    bs = glu.shape[0]
    hist = CONF_WIDTH - 1
    return pl.pallas_call(
        _conf_step_kernel,
        out_shape=[jax.ShapeDtypeStruct((bs, D_CONF), F32),
                   jax.ShapeDtypeStruct((hist, bs, D_CONF), F32)],
        compiler_params=pltpu.CompilerParams(vmem_limit_bytes=VMEM_LIMIT),
        name="conformer_step",
    )(glu, buf_tap_major, p["conf_conv_w"], p["conf_conv_b"], p["conf_ln_g"], p["conf_ln_b"], p["pw_bf16"])


def _att_kernel(bias_ref, q_ref, kb_ref, vb_ref, ga_ref, o_ref, acc_ref, nrsp_ref, z0_ref, z1_ref, w0_ref, w1_ref,
                *, tq, hps):
    grp = pl.program_id(1)
    qi = pl.program_id(2)
    hd = ATT_HEAD_DIM

    q = (q_ref[...] * (hd ** -0.5 * LOG2E)).astype(BF16)
    row = lax.broadcasted_iota(jnp.int32, (tq, tq), 0)
    col = lax.broadcasted_iota(jnp.int32, (tq, tq), 1)
    causal = col < row
    later = (row > col).astype(BF16)
    lane = lax.broadcasted_iota(jnp.int32, (tq, hd), 1)
    ones_rows = jnp.ones((hd, tq), BF16)
    qhs = []
    for hh in range(hps):
        parts = _split3(jnp.full((tq, hd), bias_ref[grp * hps + hh] * LOG2E, F32))
        extra = jnp.zeros((tq, hd), F32)
        for idx, part in enumerate(parts):
            extra = jnp.where(lane == idx, part.astype(F32), extra)
        qhs.append(jnp.concatenate([q[:, hh * hd:(hh + 1) * hd], extra.astype(BF16)], axis=1))

    heads = range(hps)
    rows = [slice(hh * hd, (hh + 1) * hd) for hh in heads]

    z_refs = (z0_ref, z1_ref)
    w_refs = (w0_ref, w1_ref)

    def logits(hh, j, slot):
        z_refs[slot][hh] = _dot(qhs[hh], jnp.concatenate([kb_ref[j, rows[hh], :], ones_rows], axis=0))

    def accumulate(hh, j, slot):
        acc_ref[hh] = acc_ref[hh] + _dot_nt(w_refs[slot][hh], vb_ref[j, rows[hh], :])

    def weights(hh, masked, slot):
        for r in range(tq // ATT_ROW_BLOCK):
            rs = slice(r * ATT_ROW_BLOCK, (r + 1) * ATT_ROW_BLOCK)
            z = z_refs[slot][hh, rs, :]
            neg_abs = jnp.minimum(z, -z)
            sp = jnp.maximum(z, 0.0) + jnp.log(1.0 + jnp.exp2(neg_abs)) * LOG2E
            nrsp = nrsp_ref[hh, rs, :]
            lsig = z - sp + nrsp
            if masked:
                sp = jnp.where(causal[rs, :], sp, 0.0)
            csp = _dot(sp.astype(BF16), later)
            w = jnp.exp2(lsig - csp)
            if masked:
                w = jnp.where(causal[rs, :], w, 0.0)
            w_refs[slot][hh, rs, :] = w.astype(BF16)
            nrsp_ref[hh, rs, :] = nrsp - (csp[:, 0:1] + sp[:, 0:1])

    acc_ref[...] = jnp.zeros_like(acc_ref)
    nrsp_ref[...] = jnp.zeros_like(nrsp_ref)
    for hh in heads:
        logits(hh, qi, 0)
    for hh in heads:
        weights(hh, True, 0)
        logits(hh, jnp.maximum(qi - 1, 0), 1)

    def step(j, slot):
        for hh in heads:
            accumulate(hh, j + 1, 1 - slot)
            logits(hh, jnp.maximum(j - 1, 0), 1 - slot)
            weights(hh, False, slot)

    @pl.loop(0, qi // 2)
    def _(it):
        j = qi - 1 - 2 * it
        step(j, 1)
        step(j - 1, 0)

    @pl.when(qi % 2 == 1)
    def _():
        step(0, 1)

    for parity in range(2):
        @pl.when(qi % 2 == parity)
        def _():
            for hh in heads:
                accumulate(hh, 0, parity)
    o = jnp.concatenate([acc_ref[hh] for hh in heads], axis=1)
    o_ref[...] = (o * _silu(ga_ref[...])).astype(o_ref.dtype)


def _att_call(q, kb, vb, ga, bias, *, nb, t, tq):
    nq = t // tq
    hps = ATT_HEADS_PER_STEP
    width = hps * ATT_HEAD_DIM
    qspec = pl.BlockSpec((tq, width), lambda b, g, i: (b * nq + i, g))
    kspec = pl.BlockSpec((None, nq, width, tq), lambda b, g, i: (b, 0, g, 0))
    return pl.pallas_call(
        functools.partial(_att_kernel, tq=tq, hps=hps),
        grid=(nb, N_ATT_HEADS // hps, nq),
        in_specs=[pl.BlockSpec(memory_space=pltpu.SMEM), qspec, kspec, kspec, qspec],
        out_specs=qspec,
        out_shape=jax.ShapeDtypeStruct((nb * t, D_ATT), ACT_DTYPE),
        scratch_shapes=[pltpu.VMEM((hps, tq, ATT_HEAD_DIM), F32), pltpu.VMEM((hps, tq, 1), F32),
                        pltpu.VMEM((hps, tq, tq), F32), pltpu.VMEM((hps, tq, tq), F32),
                        pltpu.VMEM((hps, tq, tq), BF16), pltpu.VMEM((hps, tq, tq), BF16)],
        compiler_params=_cparams("arbitrary", "arbitrary", "arbitrary"),
        name="stickbreak_prompt",
    )(bias, q, kb, vb, ga)


def _att_dec_kernel(pt_ref, bias_ref, q_ref, ga_ref, *refs, npg):
    del pt_ref
    k_refs = refs[:npg]
    v_refs = refs[npg:2 * npg]
    o_ref, acc_ref, rsp_ref = refs[2 * npg:]
    j = pl.program_id(1)
    nh = N_ATT_HEADS

    @pl.when(j == 0)
    def _():
        acc_ref[...] = jnp.zeros_like(acc_ref)
        rsp_ref[...] = jnp.zeros_like(rsp_ref)

    head = lax.broadcasted_iota(jnp.int32, (nh, D_ATT), 0)
    lane = lax.broadcasted_iota(jnp.int32, (nh, D_ATT), 1)
    own = lane // ATT_HEAD_DIM == head
    q = q_ref[...] * (ATT_HEAD_DIM ** -0.5)
    qbd = jnp.where(own, jnp.broadcast_to(q, (nh, D_ATT)), 0.0).astype(BF16)
    bias = bias_ref[...]
    zs = [_dot(qbd, k_refs[i][...].astype(BF16)) + bias for i in range(npg)]
    sps = [_softplus(z) for z in zs]
    row = lax.broadcasted_iota(jnp.int32, (PAGE_SIZE, PAGE_SIZE), 0)
    col = lax.broadcasted_iota(jnp.int32, (PAGE_SIZE, PAGE_SIZE), 1)
    later = (row > col).astype(BF16)
    loc = _dot_exact_rhs(jnp.concatenate(sps, axis=0), later)
    run = rsp_ref[...]
    acc = acc_ref[...]
    for i in reversed(range(npg)):
        w = jnp.exp(zs[i] - sps[i] - loc[i * nh:(i + 1) * nh, :] - run)
        acc = acc + _dot_nt(w.astype(BF16), v_refs[i][...].astype(BF16))
        run = run + jnp.sum(sps[i], axis=-1, keepdims=True)
    acc_ref[...] = acc
    rsp_ref[...] = run

    @pl.when(j == pl.num_programs(1) - 1)
    def _():
        o = jnp.sum(jnp.where(own, acc, 0.0), axis=0, keepdims=True)
        o_ref[...] = o * _silu(ga_ref[...])


def _att_dec_call(q, ga, cache_k, cache_v, page_table, bias, layer):
    bs, n_pages = page_table.shape
    npg = min(PAGES_PER_STEP, n_pages)
    ng = n_pages // npg
    depth, n_pool = cache_k.shape[:2]
    ck = jnp.transpose(cache_k, (0, 1, 3, 4, 2)).reshape(depth, n_pool, D_ATT, PAGE_SIZE)
    cv = jnp.transpose(cache_v, (0, 1, 3, 4, 2)).reshape(depth, n_pool, D_ATT, PAGE_SIZE)

    def page_spec(i):
        return pl.BlockSpec((None, None, D_ATT, PAGE_SIZE),
                            lambda b, j, pt: (layer, pt[b, (ng - 1 - j) * npg + i], 0, 0))

    per_b = lambda b, j, pt: (b, 0, 0)
    grid_spec = pltpu.PrefetchScalarGridSpec(
        num_scalar_prefetch=1,
        grid=(bs, ng),
        in_specs=[pl.BlockSpec((N_ATT_HEADS, 1), lambda b, j, pt: (0, 0)),
                  pl.BlockSpec((None, 1, D_ATT), per_b), pl.BlockSpec((None, 1, D_ATT), per_b)]
                 + [page_spec(i) for i in range(npg)] * 2,
        out_specs=pl.BlockSpec((None, 1, D_ATT), per_b),
        scratch_shapes=[pltpu.VMEM((N_ATT_HEADS, D_ATT), F32), pltpu.VMEM((N_ATT_HEADS, 1), F32)],
    )
    out = pl.pallas_call(
        functools.partial(_att_dec_kernel, npg=npg),
        grid_spec=grid_spec,
        out_shape=jax.ShapeDtypeStruct((bs, 1, D_ATT), F32),
        compiler_params=_cparams("arbitrary", "arbitrary"),
        name="stickbreak_paged",
    )(page_table, bias.reshape(N_ATT_HEADS, 1), q.reshape(bs, 1, D_ATT), ga.reshape(bs, 1, D_ATT),
      *([ck] * npg), *([cv] * npg))
    return out.reshape(bs, D_ATT)


def _outproj_kernel(ys_ref, yc_ref, ya_ref, x_ref, gate_ref, w_ref, fg_ref, o_ref, *, final):
    acc = _dot(ys_ref[...].astype(BF16), w_ref[0:D_SSD, :])
    acc = acc + _dot(yc_ref[...].astype(BF16), w_ref[D_SSD:D_SSD + D_CONF, :])
    acc = acc + _dot(ya_ref[...].astype(BF16), w_ref[D_SSD + D_CONF:, :])
    xn = x_ref[...] + gate_ref[...] * acc
    if final:
        xn = xn * lax.rsqrt(jnp.mean(xn * xn, axis=-1, keepdims=True) + EPS) * fg_ref[...]
    o_ref[...] = xn


def _outproj_call(ys, yc, ya, x, mod, w_bf16, final_g, *, tm, rows_per_batch, final):
    n, d = x.shape
    if rows_per_batch > 1:
        per = rows_per_batch // tm
        gate_spec = pl.BlockSpec((None, 1, d), lambda i: (i // per, 0, 2))
    else:
        gate_spec = pl.BlockSpec((tm, d), lambda i: (i, 2))
    row = lambda i: (i, 0)
    return pl.pallas_call(
        functools.partial(_outproj_kernel, final=final),
        grid=(n // tm,),
        in_specs=[pl.BlockSpec((tm, D_SSD), row), pl.BlockSpec((tm, D_CONF), row), pl.BlockSpec((tm, D_ATT), row),
                  pl.BlockSpec((tm, d), row), gate_spec,
                  pl.BlockSpec((D_SSD + D_CONF + D_ATT, d), lambda i: (0, 0)),
                  pl.BlockSpec((1, d), lambda i: (0, 0))],
        out_specs=pl.BlockSpec((tm, d), row),
        out_shape=jax.ShapeDtypeStruct((n, d), F32),
        compiler_params=_cparams("arbitrary"),
        name="outproj_residual",
    )(ys, yc, ya, x, mod, w_bf16, final_g.reshape(1, d))


def _prep_layer(l, w_in, ssd_conv_w, ssd_conv_b, ssd_dt_bias, ssd_a_log, ssd_d, ssd_norm_g, conf_conv_w,
                conf_conv_b, conf_ln_g, conf_ln_b, conf_w_pw, w_out):
    wt = jnp.swapaxes(w_in, 1, 2)[l]
    pad16 = lambda v: jnp.pad(v, (0, DT_PAD - N_SSD_HEADS)).reshape(1, DT_PAD)
    rep = lambda v: jnp.repeat(v, SSD_HEAD_DIM).reshape(1, D_SSD)
    return dict(
        w_in=wt.astype(BF16),
        ssd_conv_w=ssd_conv_w[l], ssd_conv_b=ssd_conv_b[l].reshape(1, -1),
        dtb_pad=pad16(ssd_dt_bias[l]), alog_pad=pad16(ssd_a_log[l]), alog_exp=rep(ssd_a_log[l]),
        d_exp=rep(ssd_d[l]), ssd_norm_g=ssd_norm_g[l].reshape(1, -1),
        conf_conv_w=conf_conv_w[l], conf_conv_b=conf_conv_b[l].reshape(1, -1),
        conf_ln_g=conf_ln_g[l].reshape(1, -1), conf_ln_b=conf_ln_b[l].reshape(1, -1),
        pw_bf16=conf_w_pw[l].astype(BF16), w_out=w_out[l].astype(BF16))


def kernel(x_prompt, x_sample, c_prompt, c_sample, cache_k, cache_v, page_table, state_ssm, state_ssd_conv, state_conf_conv, norm_g, w_ada, b_ada, w_in, ssd_conv_w, ssd_conv_b, ssd_dt_bias, ssd_a_log, ssd_d, ssd_norm_g, conf_conv_w, conf_conv_b, conf_ln_g, conf_ln_b, conf_w_pw, att_logit_bias, w_out, final_norm_g):
    nb, t, d = x_prompt.shape
    bs = x_sample.shape[0]
    depth = w_in.shape[0]

    n_mod = nb + bs
    n_mod_pad = -(-n_mod // 8) * 8
    c_all = jnp.concatenate([c_prompt, c_sample, jnp.zeros((n_mod_pad - n_mod, d), F32)], axis=0)
    mod = _ada_call(c_all, w_ada, b_ada)

    lane_head = jnp.arange(D_SSD) // SSD_HEAD_DIM
    expand = (jnp.arange(DT_PAD)[:, None] == lane_head[None, :]).astype(BF16)

    xp = x_prompt.reshape(nb * t, d)
    xs = x_sample.reshape(bs, d)
    tm_p = min(ATT_TILE, t)
    outs_p = []
    outs_s = []
    kv_stack = None
    h_s_all = None
    for l in range(depth):
        p = _prep_layer(l, w_in, ssd_conv_w, ssd_conv_b, ssd_dt_bias, ssd_a_log, ssd_d, ssd_norm_g,
                        conf_conv_w, conf_conv_b, conf_ln_g, conf_ln_b, conf_w_pw, w_out)
        final = l == depth - 1
        mod_p = mod[l, :nb].reshape(nb, 1, 3 * d)
        mod_s = mod[l, nb:nb + bs]

        z, xbc, glu, q, kt_all, vt_all, ga, dt, kb, vb = _inproj_call(
            xp, mod_p, norm_g[l], p["w_in"], tm=tm_p, rows_per_batch=t, layer=l, depth=depth, kv_stack=kv_stack)
        kv_stack = (kt_all, vt_all)
        y_ssd, h_p, cv_p = _ssd_call(xbc, dt, z, p, expand, nb=nb, t=t)
        y_conf, cf_p = _conf_call(glu, p, nb=nb, t=t)
        y_att = _att_call(q, kb, vb, ga, att_logit_bias[l], nb=nb, t=t, tq=tm_p)
        xp = _outproj_call(y_ssd, y_conf, y_att, xp, mod_p, p["w_out"], final_norm_g, tm=min(OUT_TILE, t),
                           rows_per_batch=t, final=final)
        outs_p.append((h_p.reshape(nb, N_SSD_HEADS, SSD_HEAD_DIM, SSD_STATE), cv_p, cf_p))

        z, xbc, glu, q, k, v, ga, dt = _inproj_call(xs, mod_s, norm_g[l], p["w_in"], tm=bs, rows_per_batch=1)
        y_ssd, h_s_all, cv_s = _ssd_step_call(xbc, dt, z, state_ssd_conv[l], state_ssm, p, expand, layer=l,
                                              h_stack=h_s_all)
        y_conf, cf_s = _conf_step_call(glu, jnp.swapaxes(state_conf_conv[l], 0, 1), p)
        y_att = _att_dec_call(q, ga, cache_k, cache_v, page_table, att_logit_bias[l], l)
        xs = _outproj_call(y_ssd, y_conf, y_att, xs, mod_s, p["w_out"], final_norm_g, tm=bs,
                           rows_per_batch=1, final=final)
        outs_s.append((k.reshape(bs, 1, N_ATT_HEADS, ATT_HEAD_DIM), v.reshape(bs, 1, N_ATT_HEADS, ATT_HEAD_DIM),
                       cv_s, jnp.swapaxes(cf_s, 0, 1)))

    stack = lambda outs, i: jnp.stack([o[i] for o in outs])
    heads_last = lambda a: jnp.transpose(a.reshape(depth, nb, N_ATT_HEADS, ATT_HEAD_DIM, t), (0, 1, 4, 2, 3))
    return (xp.reshape(nb, t, d), xs.reshape(bs, 1, d),
            heads_last(kv_stack[0]), heads_last(kv_stack[1]), stack(outs_s, 0), stack(outs_s, 1),
            stack(outs_p, 0), h_s_all.reshape(depth, bs, N_SSD_HEADS, SSD_HEAD_DIM, SSD_STATE),
            stack(outs_p, 1), stack(outs_s, 2), stack(outs_p, 2), stack(outs_s, 3))
```
